```python
import jax, jax.numpy as jnp
from jax import lax
import numpy as np

D_MODEL = 2048
BATCH = 2
SEQ = 4096
DEPTH = 4
DEC_BATCH = 32
DEC_SEQ = 16
PAST_LEN = 2048

CHUNK = 64
N_A = DEPTH // 2
N_B = DEPTH - N_A
N_DENSE = (DEPTH + 1) // 2
N_MOE = DEPTH // 2
RWKV_HEAD = 64
RWKV_HEADS = D_MODEL // RWKV_HEAD
DECAY_LORA = 96
AAA_LORA = 96
MV_LORA = 64
GATE_LORA = 256
GN_EPS = 64e-5
ATT_HEAD_DIM = 64
ATT_HEADS = D_MODEL // ATT_HEAD_DIM
PREV_CHUNKS = 8
PREV = PREV_CHUNKS * CHUNK
BAND = PREV + CHUNK
REL_CLIP = 128
D_FF = 5632
N_EXPERTS = 8
TOP_K = 2
ALPHA = (2 * DEPTH) ** 0.25
BETA = (8 * DEPTH) ** -0.25
LN_EPS = 1e-5
NEG_INF = -1e30

kernel_name = 'yoco_rwkv7_chunkband_streaming_step'


def layer_norm(x, g, b):
    xf = x.astype(jnp.float32)
    mu = jnp.mean(xf, axis=-1, keepdims=True)
    var = jnp.mean(jnp.square(xf - mu), axis=-1, keepdims=True)
    return ((xf - mu) * lax.rsqrt(var + LN_EPS) * g + b).astype(x.dtype)


def swiglu(x, w1, w3, w2):
    return (jax.nn.silu(x @ w1) * (x @ w3)) @ w2


def moe_ffn(x, router, w1, w3, w2):
    probs = jax.nn.softmax((x @ router).astype(jnp.float32), axis=-1)
    top_p, top_i = lax.top_k(probs, TOP_K)
    top_p = top_p / jnp.sum(top_p, axis=-1, keepdims=True)
    gates = jnp.sum(jax.nn.one_hot(top_i, N_EXPERTS, dtype=jnp.float32) * top_p[..., None], axis=-2).astype(x.dtype)
    y = jnp.zeros_like(x)
    for e in range(N_EXPERTS):
        y = y + gates[..., e:e + 1] * swiglu(x, w1[e], w3[e], w2[e])
    return y


def wkv_scan(r, w, k, v, a, b, s0):
    def step(s, inp):
        r_t, w_t, k_t, v_t, a_t, b_t = inp
        sa = jnp.einsum('bhij,bhj->bhi', s, a_t)
        s = s * w_t[:, :, None, :] + sa[..., None] * b_t[:, :, None, :] + v_t[..., None] * k_t[:, :, None, :]
        y = jnp.einsum('bhij,bhj->bhi', s, r_t)
        return s, y
    xs = tuple(jnp.moveaxis(t, 1, 0) for t in (r, w, k, v, a, b))
    s_final, ys = lax.scan(step, s0, xs)
    return jnp.moveaxis(ys, 0, 1), s_final


def rwkv_time_mix(x, prev_row, s0, v_first, v_lora, mu, wr, wk, wv, wo, w0, w1, w2,
                  a0, a1, a2, g1, g2, k_k, k_a, r_k, lnx_g, lnx_b):
    bsz, t, d = x.shape
    f32 = jnp.float32
    heads = lambda z: z.reshape(bsz, t, RWKV_HEADS, RWKV_HEAD)
    xx = jnp.concatenate([prev_row[:, None, :], x[:, :-1]], axis=1) - x
    xr, xw, xk, xv, xa, xg = (x + xx * mu[i] for i in range(6))
    r = xr @ wr
    k = xk @ wk
    v = xv @ wv
    w_log = -jax.nn.softplus(-(w0 + jnp.tanh(xw @ w1) @ w2).astype(f32)) - 0.5
    decay = jnp.exp(-jnp.exp(w_log))
    if v_lora is None:
        v_first = v
    else:
        v0, v1, v2 = v_lora
        v = v + (v_first - v) * jax.nn.sigmoid(v0 + (xv @ v1) @ v2)
    a = jax.nn.sigmoid(a0 + (xa @ a1) @ a2)
    g = jax.nn.sigmoid(xg @ g1) @ g2
    kk = heads((k * k_k).astype(f32))
    kk = kk / jnp.maximum(jnp.sqrt(jnp.sum(kk * kk, axis=-1, keepdims=True)), 1e-12)
    k = k * (1 + (a - 1) * k_a)
    rh, kh, vh, ah = (heads(z).astype(f32) for z in (r, k, v, a))
    y, s_final = wkv_scan(rh, heads(decay), kh, vh, -kk, kk * ah, s0.astype(f32))
    m = jnp.mean(y, axis=-1, keepdims=True)
    var = jnp.mean(jnp.square(y - m), axis=-1, keepdims=True)
    yn = ((y - m) * lax.rsqrt(var + GN_EPS)).reshape(bsz, t, d) * lnx_g + lnx_b
    bonus = (jnp.sum(rh * kh * r_k, axis=-1, keepdims=True) * vh).reshape(bsz, t, d)
    out = ((yn + bonus).astype(x.dtype) * g) @ wo
    return out, v_first, x[:, -1], s_final.astype(s0.dtype)


def rel_bias(table, dist):
    return table[:, jnp.clip(dist, -REL_CLIP, REL_CLIP) + REL_CLIP].astype(jnp.float32)


def attn_core(q, k, v, bias, mask):
    s = jnp.einsum('bqhd,bkhd->bhqk', q, k).astype(jnp.float32) * (ATT_HEAD_DIM ** -0.5) + bias
    if mask is not None:
        s = jnp.where(mask, s, NEG_INF)
    p = jax.nn.softmax(s, axis=-1).astype(v.dtype)
    return jnp.einsum('bhqk,bkhd->bqhd', p, v)


def band_attn_prompt(q, k, v, table):
    bsz, s, h, hd = q.shape
    n_chunks = s // CHUNK
    pad = jnp.zeros((bsz, PREV, h, hd), k.dtype)
    kp = jnp.concatenate([pad, k], axis=1)
    vp = jnp.concatenate([pad, v], axis=1)
    qi = jnp.arange(CHUNK)[:, None]
    kj = jnp.arange(BAND)[None, :]
    bias = rel_bias(table, qi + PREV - kj)

    def one_chunk(c):
        start = c * CHUNK
        qc = lax.dynamic_slice_in_dim(q, start, CHUNK, axis=1)
        kc = lax.dynamic_slice_in_dim(kp, start, BAND, axis=1)
        vc = lax.dynamic_slice_in_dim(vp, start, BAND, axis=1)
        mask = (start - PREV + kj) >= 0
        return attn_core(qc, kc, vc, bias, mask)

    out = lax.map(one_chunk, jnp.arange(n_chunks))
    return jnp.transpose(out, (1, 0, 2, 3, 4)).reshape(bsz, s, h, hd)


def band_attn_sample(q, cache_k, cache_v, k_new, v_new, table):
    kb = cache_k.shape[1]
    t = q.shape[1]
    k_all = jnp.concatenate([cache_k, k_new], axis=1)
    v_all = jnp.concatenate([cache_v, v_new], axis=1)
    dist = (kb + jnp.arange(t))[:, None] - jnp.arange(kb + t)[None, :]
    return attn_core(q, k_all, v_all, rel_bias(table, dist), None)


def run_trunk(x, shift_in, wkv_in, cache_k, cache_v, P):
    bsz, t, d = x.shape
    v_first = None
    k_sh = None
    v_sh = None
    shifts, wkvs = [], []
    for l in range(DEPTH):
        if l < N_A:
            v_lora = None if l == 0 else (P['rwkv_v0'][l - 1], P['rwkv_v1'][l - 1], P['rwkv_v2'][l - 1])
            out, v_first, last, s_fin = rwkv_time_mix(
                x, shift_in[l], wkv_in[l], v_first, v_lora,
                P['rwkv_mu'][l], P['rwkv_wr'][l], P['rwkv_wk'][l], P['rwkv_wv'][l], P['rwkv_wo'][l],
                P['rwkv_w0'][l], P['rwkv_w1'][l], P['rwkv_w2'][l],
                P['rwkv_a0'][l], P['rwkv_a1'][l], P['rwkv_a2'][l],
                P['rwkv_g1'][l], P['rwkv_g2'][l],
                P['rwkv_k_k'][l], P['rwkv_k_a'][l], P['rwkv_r_k'][l],
                P['rwkv_lnx_g'][l], P['rwkv_lnx_b'][l])
            shifts.append(last)
            wkvs.append(s_fin)
        else:
            i = l - N_A
            q = (x @ P['attn_wq'][i]).reshape(bsz, t, ATT_HEADS, ATT_HEAD_DIM)
            table = P['attn_rel_bias'][i]
            if cache_k is None:
                o = band_attn_prompt(q, k_sh, v_sh, table)
            else:
                o = band_attn_sample(q, cache_k, cache_v, k_sh, v_sh, table)
            out = o.reshape(bsz, t, d) @ P['attn_wo'][i]
        x = layer_norm(ALPHA * x + out, P['ln_g'][l, 0], P['ln_b'][l, 0])
        j = l // 2
        if l % 2 == 0:
            f = swiglu(x, P['ffn_w1'][j], P['ffn_w3'][j], P['ffn_w2'][j])
        else:
            f = moe_ffn(x, P['moe_router'][j], P['moe_w1'][j], P['moe_w3'][j], P['moe_w2'][j])
        x = layer_norm(ALPHA * x + f, P['ln_g'][l, 1], P['ln_b'][l, 1])
        if l == N_A - 1:
            k_sh = (x @ P['attn_wk']).reshape(bsz, t, ATT_HEADS, ATT_HEAD_DIM)
            v_sh = (x @ P['attn_wv']).reshape(bsz, t, ATT_HEADS, ATT_HEAD_DIM)
    if cache_k is None:
        keep = min(PREV, t)
        k_rows = k_sh[:, t - keep:]
        v_rows = v_sh[:, t - keep:]
    else:
        k_rows = k_sh
        v_rows = v_sh
    return x, jnp.stack(wkvs), jnp.stack(shifts), k_rows, v_rows


def setup_inputs(seed: int = 0) -> dict:
    key = jax.random.key(seed)
    ks = jax.random.split(key, 48)
    d = D_MODEL
    kb = min(PREV, PAST_LEN)
    nrm = lambda k, shape, scale: jax.random.normal(k, shape, jnp.float32) * scale
    return {
        'x_prompt': nrm(ks[0], (BATCH, SEQ, d), 1.0),
        'x_sample': nrm(ks[1], (DEC_BATCH, DEC_SEQ, d), 1.0),
        'state_wkv': nrm(ks[2], (N_A, DEC_BATCH, RWKV_HEADS, RWKV_HEAD, RWKV_HEAD), 0.5),
        'state_shift': nrm(ks[3], (N_A, DEC_BATCH, d), 1.0),
        'cache_k': nrm(ks[4], (DEC_BATCH, kb, ATT_HEADS, ATT_HEAD_DIM), 1.0),
        'cache_v': nrm(ks[5], (DEC_BATCH, kb, ATT_HEADS, ATT_HEAD_DIM), 0.5),
        'ln_g': 1.0 + nrm(ks[6], (DEPTH, 2, d), 0.02),
        'ln_b': nrm(ks[7], (DEPTH, 2, d), 0.02),
        'rwkv_mu': jax.random.uniform(ks[8], (N_A, 6, d), jnp.float32),
        'rwkv_wr': nrm(ks[9], (N_A, d, d), d ** -0.5),
        'rwkv_wk': nrm(ks[10], (N_A, d, d), d ** -0.5),
        'rwkv_wv': nrm(ks[11], (N_A, d, d), BETA * d ** -0.5),
        'rwkv_wo': nrm(ks[12], (N_A, d, d), BETA * d ** -0.5),
        'rwkv_w0': jax.random.uniform(ks[13], (N_A, d), jnp.float32, minval=-6.5, maxval=-1.0),
        'rwkv_w1': nrm(ks[14], (N_A, d, DECAY_LORA), d ** -0.5),
        'rwkv_w2': nrm(ks[15], (N_A, DECAY_LORA, d), 0.5 * DECAY_LORA ** -0.5),
        'rwkv_a0': nrm(ks[16], (N_A, d), 0.1),
        'rwkv_a1': nrm(ks[17], (N_A, d, AAA_LORA), d ** -0.5),
        'rwkv_a2': nrm(ks[18], (N_A, AAA_LORA, d), AAA_LORA ** -0.5),
        'rwkv_v0': 1.0 + nrm(ks[19], (N_A - 1, d), 0.1),
        'rwkv_v1': nrm(ks[20], (N_A - 1, d, MV_LORA), d ** -0.5),
        'rwkv_v2': nrm(ks[21], (N_A - 1, MV_LORA, d), MV_LORA ** -0.5),
        'rwkv_g1': nrm(ks[22], (N_A, d, GATE_LORA), d ** -0.5),
        'rwkv_g2': nrm(ks[23], (N_A, GATE_LORA, d), GATE_LORA ** -0.5),
        'rwkv_k_k': 0.85 + nrm(ks[24], (N_A, d), 0.05),
        'rwkv_k_a': 1.0 + nrm(ks[25], (N_A, d), 0.05),
        'rwkv_r_k': nrm(ks[26], (N_A, RWKV_HEADS, RWKV_HEAD), 0.1),
        'rwkv_lnx_g': 1.0 + nrm(ks[27], (N_A, d), 0.02),
        'rwkv_lnx_b': nrm(ks[28], (N_A, d), 0.02),
        'attn_wk': nrm(ks[29], (d, d), d ** -0.5),
        'attn_wv': nrm(ks[30], (d, d), BETA * d ** -0.5),
        'attn_wq': nrm(ks[31], (N_B, d, d), d ** -0.5),
        'attn_wo': nrm(ks[32], (N_B, d, d), BETA * d ** -0.5),
        'attn_rel_bias': nrm(ks[33], (N_B, ATT_HEADS, 2 * REL_CLIP + 1), 0.5),
        'ffn_w1': nrm(ks[34], (N_DENSE, d, D_FF), d ** -0.5),
        'ffn_w3': nrm(ks[35], (N_DENSE, d, D_FF), d ** -0.5),
        'ffn_w2': nrm(ks[36], (N_DENSE, D_FF, d), BETA * D_FF ** -0.5),
        'moe_router': nrm(ks[37], (N_MOE, d, N_EXPERTS), d ** -0.5),
        'moe_w1': nrm(ks[38], (N_MOE, N_EXPERTS, d, D_FF), d ** -0.5),
        'moe_w3': nrm(ks[39], (N_MOE, N_EXPERTS, d, D_FF), d ** -0.5),
        'moe_w2': nrm(ks[40], (N_MOE, N_EXPERTS, D_FF, d), BETA * D_FF ** -0.5),
    }


def reference(x_prompt, x_sample, state_wkv, state_shift, cache_k, cache_v, ln_g, ln_b,
              rwkv_mu, rwkv_wr, rwkv_wk, rwkv_wv, rwkv_wo, rwkv_w0, rwkv_w1, rwkv_w2,
              rwkv_a0, rwkv_a1, rwkv_a2, rwkv_v0, rwkv_v1, rwkv_v2, rwkv_g1, rwkv_g2,
              rwkv_k_k, rwkv_k_a, rwkv_r_k, rwkv_lnx_g, rwkv_lnx_b,
              attn_wk, attn_wv, attn_wq, attn_wo, attn_rel_bias,
              ffn_w1, ffn_w3, ffn_w2, moe_router, moe_w1, moe_w3, moe_w2):
    P = dict(ln_g=ln_g, ln_b=ln_b, rwkv_mu=rwkv_mu, rwkv_wr=rwkv_wr, rwkv_wk=rwkv_wk,
             rwkv_wv=rwkv_wv, rwkv_wo=rwkv_wo, rwkv_w0=rwkv_w0, rwkv_w1=rwkv_w1, rwkv_w2=rwkv_w2,
             rwkv_a0=rwkv_a0, rwkv_a1=rwkv_a1, rwkv_a2=rwkv_a2, rwkv_v0=rwkv_v0, rwkv_v1=rwkv_v1,
             rwkv_v2=rwkv_v2, rwkv_g1=rwkv_g1, rwkv_g2=rwkv_g2, rwkv_k_k=rwkv_k_k, rwkv_k_a=rwkv_k_a,
             rwkv_r_k=rwkv_r_k, rwkv_lnx_g=rwkv_lnx_g, rwkv_lnx_b=rwkv_lnx_b,
             attn_wk=attn_wk, attn_wv=attn_wv, attn_wq=attn_wq, attn_wo=attn_wo,
             attn_rel_bias=attn_rel_bias, ffn_w1=ffn_w1, ffn_w3=ffn_w3, ffn_w2=ffn_w2,
             moe_router=moe_router, moe_w1=moe_w1, moe_w3=moe_w3, moe_w2=moe_w2)
    b_p = x_prompt.shape[0]
    zero_shift = jnp.zeros((N_A, b_p, D_MODEL), x_prompt.dtype)
    zero_wkv = jnp.zeros((N_A, b_p, RWKV_HEADS, RWKV_HEAD, RWKV_HEAD), state_wkv.dtype)
    y_prompt, p_wkv, p_shift, p_k, p_v = run_trunk(x_prompt, zero_shift, zero_wkv, None, None, P)
    y_sample, s_wkv, s_shift, s_k, s_v = run_trunk(x_sample, state_shift, state_wkv, cache_k, cache_v, P)
    return (y_prompt, y_sample, p_wkv, p_shift, p_k, p_v, s_wkv, s_shift, s_k, s_v)
```

```python
import functools

import jax
import jax.numpy as jnp
from jax import lax
from jax.experimental import pallas as pl
from jax.experimental.pallas import tpu as pltpu

F32 = jnp.float32
BF16 = jnp.bfloat16

CHUNK = 64
PREV_CHUNKS = 8
PREV = PREV_CHUNKS * CHUNK
REL_CLIP = 128
HEAD = 64
TOP_K = 2
GN_EPS = 64e-5
LN_EPS = 1e-5
NEG_INF = -1e30

LANES = 128
VMEM_LIMIT_BYTES = 60 * 1024 * 1024


def _params(*sem):
    return pltpu.CompilerParams(dimension_semantics=sem, vmem_limit_bytes=VMEM_LIMIT_BYTES)


def _tile(n, target, mult=16):
    best = None
    for t in range(mult, min(n, target) + 1, mult):
        if n % t == 0:
            best = t
    assert best is not None, (n, target, mult)
    return best


def _dot(a, b):
    return jnp.dot(a, b, preferred_element_type=F32)


def _dot_nt(a, b):
    return lax.dot_general(a, b, (((1,), (1,)), ((), ())), preferred_element_type=F32)


def _dot_tn(a, b):
    return lax.dot_general(a, b, (((0,), (0,)), ((), ())), preferred_element_type=F32)


def _split_bf16(x):
    hi = x.astype(BF16)
    lo = (x - hi.astype(F32)).astype(BF16)
    return hi, lo


def _layer_norm(y, g, b):
    mu = jnp.mean(y, axis=-1, keepdims=True)
    dev = y - mu
    var = jnp.mean(dev * dev, axis=-1, keepdims=True)
    return dev * lax.rsqrt(var + LN_EPS) * g + b


def _softplus(u):
    return jnp.maximum(u, 0.0) + jnp.log(1.0 + jnp.exp(-jnp.abs(u)))


def _mm_kernel(*refs, n_extra, n_vec, epilogue, cast_w):
    x_ref, w_ref = refs[0], refs[1]
    extra = refs[2:2 + n_extra]
    vecs = refs[2 + n_extra:2 + n_extra + n_vec]
    o_ref = refs[2 + n_extra + n_vec]
    if cast_w:
        wb_ref = refs[-1]

        @pl.when(pl.program_id(1) == 0)
        def _():
            wb_ref[...] = w_ref[...].astype(BF16)

        w = wb_ref[...]
    else:
        w = w_ref[...]
    acc = _dot(x_ref[...].astype(BF16), w)
    if epilogue is not None:
        acc = epilogue(acc, *[e[...] for e in extra], *[v[...] for v in vecs])
    o_ref[...] = acc.astype(o_ref.dtype)


def _matmul(x, w, *, out_dtype=F32, epilogue=None, extras=(), vecs=(), tm=512, tn=1024):
    m, k = x.shape
    n = w.shape[1]
    tm = _tile(m, tm)
    tn = n if n <= tn else _tile(n, tn, LANES)
    cast_w = w.dtype != BF16
    in_specs = [pl.BlockSpec((tm, k), lambda j, i: (i, 0)),
                pl.BlockSpec((k, tn), lambda j, i: (0, j))]
    in_specs += [pl.BlockSpec((tm, tn), lambda j, i: (i, j)) for _ in extras]
    in_specs += [pl.BlockSpec((1, tn), lambda j, i: (0, j)) for _ in vecs]
    return pl.pallas_call(
        functools.partial(_mm_kernel, n_extra=len(extras), n_vec=len(vecs),
                          epilogue=epilogue, cast_w=cast_w),
        grid=(n // tn, m // tm),
        in_specs=in_specs,
        out_specs=pl.BlockSpec((tm, tn), lambda j, i: (i, j)),
        out_shape=jax.ShapeDtypeStruct((m, n), out_dtype),
        scratch_shapes=[pltpu.VMEM((k, tn), BF16)] if cast_w else [],
        compiler_params=_params("parallel", "arbitrary"),
    )(x, w, *extras, *vecs)


def _ln_epilogue(alpha):
    def epi(acc, xres, g, b):
        return _layer_norm(alpha * xres + acc, g, b)
    return epi


def _rwkv_pre_kernel(*refs, has_v):
    it = iter(refs)
    x_ref, xp_ref, mu_ref = next(it), next(it), next(it)
    w1, a1, g1 = next(it), next(it), next(it)
    v1 = next(it) if has_v else None
    w2, a2, g2 = next(it), next(it), next(it)
    v2 = next(it) if has_v else None
    w0, a0 = next(it), next(it)
    v0 = next(it) if has_v else None
    xr_ref, xk_ref, xv_ref, lw_ref, ag_ref, g_ref = (next(it) for _ in range(6))
    vg_ref = next(it) if has_v else None

    x = x_ref[...]
    xx = xp_ref[...] - x
    mix = lambda i: (x + xx * mu_ref[i:i + 1, :]).astype(BF16)
    xr_ref[...] = mix(0)
    xk_ref[...] = mix(2)
    xv = mix(3)
    xv_ref[...] = xv
    hw = jnp.tanh(_dot(mix(1), w1[...]))
    z = w0[...] + _dot(hw.astype(BF16), w2[...])
    lw_ref[...] = -jnp.exp(-_softplus(-z) - 0.5)
    ha = _dot(mix(4), a1[...])
    ag_ref[...] = jax.nn.sigmoid(a0[...] + _dot(ha.astype(BF16), a2[...]))
    hg = jax.nn.sigmoid(_dot(mix(5), g1[...]))
    g_ref[...] = _dot(hg.astype(BF16), g2[...])
    if has_v:
        hv = _dot(xv, v1[...])
        vg_ref[...] = jax.nn.sigmoid(v0[...] + _dot(hv.astype(BF16), v2[...]))


def _rwkv_pre(x, xprev, mu, lora1, lora2, bias, *, tm=256):
    m, d = x.shape
    tm = _tile(m, tm)
    has_v = len(lora1) == 4
    row = pl.BlockSpec((tm, d), lambda i: (i, 0))
    full = lambda a: pl.BlockSpec(a.shape, lambda i: (0,) * a.ndim)
    ins = [x, xprev, mu, *lora1, *lora2, *bias]
    in_specs = [row, row] + [full(a) for a in ins[2:]]
    out_shape = [jax.ShapeDtypeStruct((m, d), BF16)] * 3 + [jax.ShapeDtypeStruct((m, d), F32)] * (3 + has_v)
    return pl.pallas_call(
        functools.partial(_rwkv_pre_kernel, has_v=has_v),
        grid=(m // tm,),
        in_specs=in_specs,
        out_specs=[row] * len(out_shape),
        out_shape=out_shape,
        compiler_params=_params("parallel"),
    )(*ins)


def _wkv_kernel(*refs, L, npair, has_init):
    it = iter(refs)
    r_ref, k_ref, v_ref, lw_ref, ag_ref, g_ref = (next(it) for _ in range(6))
    kk_ref, ka_ref, rk_ref, lg_ref, lb_ref = (next(it) for _ in range(5))
    s0_ref = next(it) if has_init else None
    z_ref, st_ref = next(it), next(it)
    L2 = 2 * L

    @pl.when(pl.program_id(2) == 0)
    def _():
        st_ref[...] = s0_ref[...] if has_init else jnp.zeros(st_ref.shape, F32)

    def iota(shape, dim):
        return lax.broadcasted_iota(jnp.int32, shape, dim)

    tri = (iota((L, L), 0) >= iota((L, L), 1)).astype(BF16)
    lw = lw_ref[...]
    lw_hi, lw_lo = _split_bf16(lw)
    cum = _dot(tri, lw_hi) + _dot(tri, lw_lo)
    cum_last = cum[L - 1:L, :]
    e_pos = jnp.exp(cum)
    e_neg = jnp.exp(-cum)
    e_prev = jnp.exp(cum - lw)
    e_last = jnp.exp(cum_last - cum)

    rows, cols = iota((L2, L2), 0), iota((L2, L2), 1)
    same_head = (rows >= L) == (cols >= L)
    lower_strict = same_head & (rows > cols)
    lower_incl = same_head & (rows >= cols)
    eye = (rows == cols).astype(F32)
    own_lanes = ((iota((L2, LANES), 0) >= L) == (iota((L2, LANES), 1) >= HEAD)).astype(F32)
    ones = jnp.ones((L, LANES), BF16)
    bf = lambda t: t.astype(BF16)
    dup = lambda t: jnp.concatenate([t, t], axis=0)
    stack = lambda t: dup(t) * own_lanes
    unstack = lambda t: t[:L] + t[L:]

    for p in range(npair):
        sl = slice(p * LANES, (p + 1) * LANES)
        ag = ag_ref[:, sl]
        kraw = k_ref[:, sl]
        kk = stack(kraw * kk_ref[:, sl])
        kk = kk / jnp.maximum(jnp.sqrt(jnp.sum(kk * kk, axis=-1, keepdims=True)), 1e-12)
        ks = stack(kraw * (1.0 + (ag - 1.0) * ka_ref[:, sl]))
        rs = stack(r_ref[:, sl])
        vs = stack(v_ref[:, sl])
        a = -kk
        b = kk * dup(ag)
        at = bf(a * dup(e_prev[:, sl]))
        rt = rs * dup(e_pos[:, sl])
        bt = bf(b * dup(e_neg[:, sl]))
        kt = bf(ks * dup(e_neg[:, sl]))
        bh = bf(b * dup(e_last[:, sl]))
        kh = bf(ks * dup(e_last[:, sl]))
        vb = bf(vs)
        rtb = bf(rt)

        m_ab = jnp.where(lower_strict, _dot_nt(at, bt), 0.0)
        m_ak = jnp.where(lower_strict, _dot_nt(at, kt), 0.0)
        m_rb = bf(jnp.where(lower_incl, _dot_nt(rtb, bt), 0.0))
        m_rk = bf(jnp.where(lower_incl, _dot_nt(rtb, kt), 0.0))

        tinv = eye + m_ab
        mp = m_ab
        span = 2
        while span < L:
            mpb = bf(mp)
            mp = _dot(mpb, mpb)
            tinv = tinv + _dot(bf(tinv), bf(mp))
            span *= 2
        tinvb = bf(tinv)

        ah = _dot(tinvb, at)
        u0 = _dot(tinvb, bf(_dot(bf(m_ak), vb)))
        ahb, u0b = bf(ah), bf(u0)
        rh = rt + _dot(m_rb, ahb)
        y0 = _dot(m_rb, u0b) + _dot(m_rk, vb)
        gmat = _dot_tn(bh, ahb)
        qmat = _dot_tn(bh, u0b) + _dot_tn(kh, vb)
        decay = jnp.exp(_dot_tn(lw_hi[:, sl], ones) + _dot_tn(lw_lo[:, sl], ones))

        st = st_ref[0, p]
        stb = bf(st)
        ys = _dot(bf(rh), stb) + y0
        st_ref[0, p] = decay * st + _dot(bf(gmat), stb) + qmat

        mean = jnp.sum(ys, axis=-1, keepdims=True) * (1.0 / HEAD)
        dev = (ys - mean) * own_lanes
        var = jnp.sum(dev * dev, axis=-1, keepdims=True) * (1.0 / HEAD)
        yn = unstack(dev * lax.rsqrt(var + GN_EPS)) * lg_ref[:, sl] + lb_ref[:, sl]
        bonus = unstack(jnp.sum(rs * ks * rk_ref[:, sl], axis=-1, keepdims=True) * vs)
        z_ref[:, sl] = ((yn + bonus) * g_ref[:, sl]).astype(z_ref.dtype)


def _wkv(seqs, vecs, s0, *, n_seq, seq_len, row0, L, hw):
    d = seqs[0].shape[1]
    hw = min(hw, d)
    nc = seq_len // L
    blk0 = row0 // L
    npair = hw // LANES
    seq_spec = pl.BlockSpec((L, hw), lambda b, h, c: (blk0 + b * nc + c, h))
    vec_spec = pl.BlockSpec((1, hw), lambda b, h, c: (0, h))
    st_spec = pl.BlockSpec((1, npair, LANES, LANES), lambda b, h, c: (b, h, 0, 0))
    in_specs = [seq_spec] * 6 + [vec_spec] * 5 + ([st_spec] if s0 is not None else [])
    ins = list(seqs) + list(vecs) + ([s0] if s0 is not None else [])
    return pl.pallas_call(
        functools.partial(_wkv_kernel, L=L, npair=npair, has_init=s0 is not None),
        grid=(n_seq, d // hw, nc),
        in_specs=in_specs,
        out_specs=[pl.BlockSpec((L, hw), lambda b, h, c: (b * nc + c, h)), st_spec],
        out_shape=[jax.ShapeDtypeStruct((n_seq * seq_len, d), BF16),
                   jax.ShapeDtypeStruct((n_seq, d // LANES, LANES, LANES), F32)],
        compiler_params=_params("parallel", "parallel", "arbitrary"),
    )(*ins)


def _state_to_pairs(s):
    b, h, n, _ = s.shape
    st = jnp.swapaxes(s, -1, -2).reshape(b, h // 2, 2, n, n)
    z = jnp.zeros_like(st[:, :, 0])
    top = jnp.concatenate([st[:, :, 0], z], axis=-1)
    bot = jnp.concatenate([z, st[:, :, 1]], axis=-1)
    return jnp.concatenate([top, bot], axis=-2)


def _pairs_to_state(sp):
    b, hp = sp.shape[:2]
    s = jnp.stack([sp[:, :, :HEAD, :HEAD], sp[:, :, HEAD:, HEAD:]], axis=2)
    return jnp.swapaxes(s.reshape(b, hp * 2, HEAD, HEAD), -1, -2)


def _head_masks():
    lane = lax.broadcasted_iota(jnp.int32, (1, LANES), 1)
    return lane < HEAD


def _attn_prompt_kernel(q_ref, k0, k1, k2, v0, v1, v2, bias_ref, o_ref, *, qb):
    first = _head_masks()
    blk = pl.program_id(2)
    q = q_ref[...]
    kcat = jnp.concatenate([k0[...], k1[...], k2[...]], axis=0).astype(BF16)
    vcat = jnp.concatenate([v0[...], v1[...], v2[...]], axis=0).astype(BF16)
    kpos = lax.broadcasted_iota(jnp.int32, (1, 3 * qb), 1) + (blk - 2) * qb
    valid = kpos >= 0
    outs = []
    for h in range(2):
        qh = jnp.where(first if h == 0 else ~first, q, 0.0).astype(BF16)
        s = _dot_nt(qh, kcat) * (HEAD ** -0.5) + bias_ref[h]
        s = jnp.where(valid, s, NEG_INF)
        p = jnp.exp(s - jnp.max(s, axis=-1, keepdims=True))
        denom = jnp.sum(p, axis=-1, keepdims=True)
        outs.append(_dot(p.astype(BF16), vcat) / denom)
    o_ref[...] = jnp.where(first, outs[0], outs[1]).astype(o_ref.dtype)


def _attn_prompt(q, k, v, bias, *, n_seq, seq_len, qb):
    d = q.shape[1]
    nq = seq_len // qb
    q_spec = pl.BlockSpec((qb, LANES), lambda h, b, i: (b * nq + i, h))
    kv = lambda off: pl.BlockSpec((qb, LANES), lambda h, b, i: (b * nq + jnp.maximum(i - off, 0), h))
    return pl.pallas_call(
        functools.partial(_attn_prompt_kernel, qb=qb),
        grid=(d // LANES, n_seq, nq),
        in_specs=[q_spec, kv(2), kv(1), kv(0), kv(2), kv(1), kv(0),
                  pl.BlockSpec((2, qb, 3 * qb), lambda h, b, i: (h, 0, 0))],
        out_specs=pl.BlockSpec((qb, LANES), lambda h, b, i: (b * nq + i, h)),
        out_shape=jax.ShapeDtypeStruct((n_seq * seq_len, d), BF16),
        compiler_params=_params("parallel", "parallel", "arbitrary"),
    )(q, k, k, k, v, v, v, bias)


def _attn_sample_kernel(q_ref, kn_ref, vn_ref, kc_ref, vc_ref, bc_ref, bn_ref, o_ref, *, npair):
    first = _head_masks()
    for p in range(npair):
        sl = slice(p * LANES, (p + 1) * LANES)
        q = q_ref[:, sl]
        kc, vc = kc_ref[0, :, sl].astype(BF16), vc_ref[0, :, sl].astype(BF16)
        kn, vn = kn_ref[:, sl].astype(BF16), vn_ref[:, sl].astype(BF16)
        outs = []
        for h in range(2):
            qh = jnp.where(first if h == 0 else ~first, q, 0.0).astype(BF16)
            sc = _dot_nt(qh, kc) * (HEAD ** -0.5) + bc_ref[2 * p + h]
            sn = _dot_nt(qh, kn) * (HEAD ** -0.5) + bn_ref[2 * p + h]
            mx = jnp.maximum(jnp.max(sc, axis=-1, keepdims=True), jnp.max(sn, axis=-1, keepdims=True))
            pc, pn = jnp.exp(sc - mx), jnp.exp(sn - mx)
            denom = jnp.sum(pc, axis=-1, keepdims=True) + jnp.sum(pn, axis=-1, keepdims=True)
            outs.append((_dot(pc.astype(BF16), vc) + _dot(pn.astype(BF16), vn)) / denom)
        o_ref[:, sl] = jnp.where(first, outs[0], outs[1]).astype(o_ref.dtype)


def _attn_sample(q, k, v, cache_k, cache_v, bias_c, bias_n, *, n_seq, seq_len, row0):
    d = q.shape[1]
    kb = cache_k.shape[1]
    blk0 = row0 // seq_len
    new = pl.BlockSpec((seq_len, d), lambda s: (blk0 + s, 0))
    cache = pl.BlockSpec((1, kb, d), lambda s: (s, 0, 0))
    full = lambda a: pl.BlockSpec(a.shape, lambda s: (0,) * a.ndim)
    return pl.pallas_call(
        functools.partial(_attn_sample_kernel, npair=d // LANES),
        grid=(n_seq,),
        in_specs=[new, new, new, cache, cache, full(bias_c), full(bias_n)],
        out_specs=pl.BlockSpec((seq_len, d), lambda s: (s, 0)),
        out_shape=jax.ShapeDtypeStruct((n_seq * seq_len, d), BF16),
        compiler_params=_params("parallel"),
    )(q, k, v, cache_k, cache_v, bias_c, bias_n)


def _prompt_bias(table, qb):
    qi = jnp.arange(qb)[:, None]
    kj = jnp.arange(3 * qb)[None, :] - 2 * qb
    dist = qi - kj
    qc, kc = qi // CHUNK, jnp.floor_divide(kj, CHUNK)
    band = (kc <= qc) & (kc >= qc - PREV_CHUNKS)
    bias = table[:, jnp.clip(dist, -REL_CLIP, REL_CLIP) + REL_CLIP].astype(F32)
    return jnp.where(band[None], bias, NEG_INF)


def _sample_bias(table, kb, t):
    dist = (kb + jnp.arange(t))[:, None] - jnp.arange(kb + t)[None, :]
    bias = table[:, jnp.clip(dist, -REL_CLIP, REL_CLIP) + REL_CLIP].astype(F32)
    return bias[:, :, :kb], bias[:, :, kb:]


def _ffn_kernel(te_ref, na_ref, x_ref, w1_ref, w3_ref, w2_ref, *rest, nf, alpha, fuse_ln):
    if fuse_ln:
        g_ref, b_ref, o_ref, xb_ref = rest
    else:
        gate_ref, o_ref = rest
    t, f = pl.program_id(0), pl.program_id(1)
    active = t < na_ref[0]

    @pl.when(active)
    def _():
        if fuse_ln:
            @pl.when(f == 0)
            def _():
                xb_ref[...] = x_ref[...].astype(BF16)
            xb = xb_ref[...]
        else:
            xb = x_ref[...]
        h1 = _dot(xb, w1_ref[0].astype(BF16))
        h3 = _dot(xb, w3_ref[0].astype(BF16))
        h = (h1 * jax.nn.sigmoid(h1) * h3).astype(BF16)
        contrib = _dot(h, w2_ref[0].astype(BF16))

        @pl.when(f == 0)
        def _():
            o_ref[...] = contrib

        @pl.when(f > 0)
        def _():
            o_ref[...] += contrib

    @pl.when(f == nf - 1)
    def _():
        if fuse_ln:
            o_ref[...] = _layer_norm(alpha * x_ref[...] + o_ref[...], g_ref[...], b_ref[...])
        else:
            @pl.when(active)
            def _():
                o_ref[...] = o_ref[...] * gate_ref[...]

            @pl.when(jnp.logical_not(active))
            def _():
                o_ref[...] = jnp.zeros(o_ref.shape, F32)


def _ffn(x, w1, w3, w2, tile_expert, n_active, *, tm, tf, gate=None, ln=None, alpha=None):
    rows, d = x.shape
    dff = w1.shape[2]
    tf = _tile(dff, tf, LANES)
    nf = dff // tf
    fuse_ln = ln is not None

    def clamp(t, na):
        return jnp.minimum(t, na[0] - 1)

    def f_idx(t, f, na):
        return jnp.where(t < na[0], f, nf - 1)

    x_spec = pl.BlockSpec((tm, d), lambda t, f, te, na: (clamp(t, na), 0))
    w13 = pl.BlockSpec((1, d, tf), lambda t, f, te, na: (te[clamp(t, na)], 0, f_idx(t, f, na)))
    w2s = pl.BlockSpec((1, tf, d), lambda t, f, te, na: (te[clamp(t, na)], f_idx(t, f, na), 0))
    o_spec = pl.BlockSpec((tm, d), lambda t, f, te, na: (t, 0))
    if fuse_ln:
        vec = pl.BlockSpec((1, d), lambda t, f, te, na: (0, 0))
        extra, extra_specs = list(ln), [vec, vec]
        scratch = [pltpu.VMEM((tm, d), BF16)]
    else:
        extra, extra_specs = [gate], [pl.BlockSpec((tm, 1), lambda t, f, te, na: (t, 0))]
        scratch = []
    return pl.pallas_call(
        functools.partial(_ffn_kernel, nf=nf, alpha=alpha, fuse_ln=fuse_ln),
        grid_spec=pltpu.PrefetchScalarGridSpec(
            num_scalar_prefetch=2,
            grid=(rows // tm, nf),
            in_specs=[x_spec, w13, w13, w2s] + extra_specs,
            out_specs=o_spec,
            scratch_shapes=scratch),
        out_shape=jax.ShapeDtypeStruct((rows, d), F32),
        compiler_params=_params("parallel", "arbitrary"),
    )(tile_expert, n_active, x, w1, w3, w2, *extra)


def _router_kernel(x_ref, w_ref, o_ref, *, n_exp):
    x_hi, x_lo = _split_bf16(x_ref[...])
    w_hi, w_lo = _split_bf16(w_ref[...])
    logits = _dot(x_hi, w_hi) + _dot(x_hi, w_lo) + _dot(x_lo, w_hi)
    lane = lax.broadcasted_iota(jnp.int32, logits.shape, 1).astype(F32)
    valid = lane < n_exp
    lg = jnp.where(valid, logits, NEG_INF)
    ex = jnp.where(valid, jnp.exp(lg - jnp.max(lg, axis=-1, keepdims=True)), 0.0)
    probs = jnp.where(valid, ex / jnp.sum(ex, axis=-1, keepdims=True), -1.0)
    p1 = jnp.max(probs, axis=-1, keepdims=True)
    i1 = jnp.min(jnp.where(probs == p1, lane, float(LANES)), axis=-1, keepdims=True)
    rest = jnp.where(lane == i1, -1.0, probs)
    p2 = jnp.max(rest, axis=-1, keepdims=True)
    i2 = jnp.min(jnp.where(rest == p2, lane, float(LANES)), axis=-1, keepdims=True)
    tot = p1 + p2
    out = jnp.where(lane == 0, i1,
                    jnp.where(lane == 1, i2,
                              jnp.where(lane == 2, p1 / tot, jnp.where(lane == 3, p2 / tot, 0.0))))
    o_ref[...] = out


def _router(x, w, *, tm=512):
    m, d = x.shape
    n_exp = w.shape[1]
    wp = jnp.zeros((d, LANES), F32).at[:, :n_exp].set(w)
    tm = _tile(m, tm)
    return pl.pallas_call(
        functools.partial(_router_kernel, n_exp=n_exp),
        grid=(m // tm,),
        in_specs=[pl.BlockSpec((tm, d), lambda i: (i, 0)), pl.BlockSpec((d, LANES), lambda i: (0, 0))],
        out_specs=pl.BlockSpec((tm, LANES), lambda i: (i, 0)),
        out_shape=jax.ShapeDtypeStruct((m, LANES), F32),
        compiler_params=_params("parallel"),
    )(x, wp)


def _row_copy(src_ref, dst_ref, sem, src_row, dst_row):
    return pltpu.make_async_copy(src_ref.at[pl.ds(src_row, 1)], dst_ref.at[pl.ds(dst_row, 1)], sem)


def _gather_in_kernel(idx_ref, x_ref, o_ref, buf, sem, *, tg):
    base = pl.program_id(0) * tg

    def start(i, c):
        _row_copy(x_ref, buf, sem, idx_ref[base + i], i).start()
        return c

    def wait(i, c):
        _row_copy(x_ref, buf, sem, 0, i).wait()
        return c

    lax.fori_loop(0, tg, start, 0)
    lax.fori_loop(0, tg, wait, 0)
    o_ref[...] = buf[...].astype(o_ref.dtype)


def _gather_in(x, idx, *, tg=256):
    d = x.shape[1]
    n = idx.shape[0]
    return pl.pallas_call(
        functools.partial(_gather_in_kernel, tg=tg),
        grid_spec=pltpu.PrefetchScalarGridSpec(
            num_scalar_prefetch=1,
            grid=(n // tg,),
            in_specs=[pl.BlockSpec(memory_space=pl.ANY)],
            out_specs=pl.BlockSpec((tg, d), lambda i, idx: (i, 0)),
            scratch_shapes=[pltpu.VMEM((tg, d), F32), pltpu.SemaphoreType.DMA(())]),
        out_shape=jax.ShapeDtypeStruct((n, d), BF16),
        compiler_params=_params("arbitrary"),
    )(idx, x)


def _combine_kernel(i0_ref, i1_ref, y_ref, x_ref, g_ref, b_ref, o_ref, buf, sem, *, tc, alpha):
    base = pl.program_id(0) * tc

    def start(i, c):
        _row_copy(y_ref, buf.at[0], sem, i0_ref[base + i], i).start()
        _row_copy(y_ref, buf.at[1], sem, i1_ref[base + i], i).start()
        return c

    def wait(i, c):
        _row_copy(y_ref, buf.at[0], sem, 0, i).wait()
        _row_copy(y_ref, buf.at[1], sem, 0, i).wait()
        return c

    lax.fori_loop(0, tc, start, 0)
    lax.fori_loop(0, tc, wait, 0)
    o_ref[...] = _layer_norm(alpha * x_ref[...] + (buf[0] + buf[1]), g_ref[...], b_ref[...])


def _combine(y, x, i0, i1, g, b, *, alpha, tc=256):
    m, d = x.shape
    tc = _tile(m, tc)
    row = pl.BlockSpec((tc, d), lambda i, a, c: (i, 0))
    vec = pl.BlockSpec((1, d), lambda i, a, c: (0, 0))
    return pl.pallas_call(
        functools.partial(_combine_kernel, tc=tc, alpha=alpha),
        grid_spec=pltpu.PrefetchScalarGridSpec(
            num_scalar_prefetch=2,
            grid=(m // tc,),
            in_specs=[pl.BlockSpec(memory_space=pl.ANY), row, vec, vec],
            out_specs=row,
            scratch_shapes=[pltpu.VMEM((2, tc, d), F32), pltpu.SemaphoreType.DMA(())]),
        out_shape=jax.ShapeDtypeStruct((m, d), F32),
        compiler_params=_params("arbitrary"),
    )(i0, i1, y, x, g, b)


def _moe(x, router, w1, w3, w2, g, b, *, alpha, tm):
    m, d = x.shape
    n_exp = w1.shape[0]
    sel = _router(x, router)
    top_i = sel[:, :TOP_K].astype(jnp.int32)
    top_p = sel[:, TOP_K:2 * TOP_K]
    e_flat = top_i.reshape(-1)
    onehot = (e_flat[:, None] == jnp.arange(n_exp)[None, :]).astype(jnp.int32)
    rank = jnp.sum((jnp.cumsum(onehot, axis=0) - onehot) * onehot, axis=1)
    counts = jnp.sum(onehot, axis=0)
    padded = (counts + tm - 1) // tm * tm
    ends = jnp.cumsum(padded)
    slot = (ends - padded)[e_flat] + rank
    n_slots = (TOP_K * m + n_exp * (tm - 1)) // tm * tm
    n_tiles = n_slots // tm
    src = jnp.zeros((n_slots,), jnp.int32).at[slot].set(jnp.arange(TOP_K * m, dtype=jnp.int32) // TOP_K)
    gate = jnp.zeros((n_slots, 1), F32).at[slot, 0].set(top_p.reshape(-1))
    tile_expert = jnp.minimum(
        jnp.searchsorted(ends, jnp.arange(n_tiles, dtype=jnp.int32) * tm, side="right"),
        n_exp - 1).astype(jnp.int32)
    n_active = (ends[-1:] // tm).astype(jnp.int32)

    xs = _gather_in(x, src, tg=_tile(tm, 256))
    ys = _ffn(xs, w1, w3, w2, tile_expert, n_active, tm=tm, tf=256, gate=gate)
    slot2 = slot.reshape(m, TOP_K).astype(jnp.int32)
    return _combine(ys, x, slot2[:, 0], slot2[:, 1], g, b, alpha=alpha)


def _dense_ffn(x, w1, w3, w2, g, b, *, alpha, tm):
    m = x.shape[0]
    n_tiles = m // tm
    return _ffn(x, w1[None].astype(BF16), w3[None].astype(BF16), w2[None].astype(BF16),
                jnp.zeros((n_tiles,), jnp.int32), jnp.full((1,), n_tiles, jnp.int32),
                tm=tm, tf=512, ln=(g, b), alpha=alpha)


def kernel(x_prompt, x_sample, state_wkv, state_shift, cache_k, cache_v, ln_g, ln_b, rwkv_mu, rwkv_wr, rwkv_wk, rwkv_wv, rwkv_wo, rwkv_w0, rwkv_w1, rwkv_w2, rwkv_a0, rwkv_a1, rwkv_a2, rwkv_v0, rwkv_v1, rwkv_v2, rwkv_g1, rwkv_g2, rwkv_k_k, rwkv_k_a, rwkv_r_k, rwkv_lnx_g, rwkv_lnx_b, attn_wk, attn_wv, attn_wq, attn_wo, attn_rel_bias, ffn_w1, ffn_w3, ffn_w2, moe_router, moe_w1, moe_w3, moe_w2):
    bp, tp, d = x_prompt.shape
    bs, ts, _ = x_sample.shape
    depth = ln_g.shape[0]
    n_a = rwkv_wr.shape[0]
    alpha = (2 * depth) ** 0.25
    mp, ms = bp * tp, bs * ts
    m = mp + ms
    kb = cache_k.shape[1]
    assert tp % CHUNK == 0 and d % LANES == 0 and mp % ts == 0
    row = lambda a: a.reshape(1, -1)
    bf = lambda a: a.astype(BF16)
    qb = _tile(tp, 256, CHUNK)
    assert 2 * qb >= PREV
    tm_dense = _tile(m, 640)
    tm_moe = 1024 if m >= 4096 else 64

    x = jnp.concatenate([x_prompt.reshape(mp, d), x_sample.reshape(ms, d)], axis=0)
    p_wkv, s_wkv, p_shift, s_shift = [], [], [], []
    v_first = k_sh = v_sh = None
    for l in range(depth):
        g0, b0, g1, b1 = row(ln_g[l, 0]), row(ln_b[l, 0]), row(ln_g[l, 1]), row(ln_b[l, 1])
        if l < n_a:
            xp3, xs3 = x[:mp].reshape(bp, tp, d), x[mp:].reshape(bs, ts, d)
            xprev = jnp.concatenate([
                jnp.concatenate([jnp.zeros((bp, 1, d), F32), xp3[:, :-1]], axis=1).reshape(mp, d),
                jnp.concatenate([state_shift[l][:, None], xs3[:, :-1]], axis=1).reshape(ms, d)], axis=0)
            p_shift.append(xp3[:, -1])
            s_shift.append(xs3[:, -1])
            has_v = l > 0
            lora1 = [bf(rwkv_w1[l]), bf(rwkv_a1[l]), bf(rwkv_g1[l])] + ([bf(rwkv_v1[l - 1])] if has_v else [])
            lora2 = [bf(rwkv_w2[l]), bf(rwkv_a2[l]), bf(rwkv_g2[l])] + ([bf(rwkv_v2[l - 1])] if has_v else [])
            bias = [row(rwkv_w0[l]), row(rwkv_a0[l])] + ([row(rwkv_v0[l - 1])] if has_v else [])
            pre = _rwkv_pre(x, xprev, rwkv_mu[l], lora1, lora2, bias)
            xr, xk, xv, lw, ag, gg = pre[:6]
            r = _matmul(xr, rwkv_wr[l])
            k = _matmul(xk, rwkv_wk[l])
            if has_v:
                v = _matmul(xv, rwkv_wv[l], extras=(v_first, pre[6]),
                            epilogue=lambda acc, vf, vg: acc + (vf - acc) * vg)
            else:
                v = _matmul(xv, rwkv_wv[l])
                v_first = v
            seqs = (r, k, v, lw, ag, gg)
            vecs = [row(rwkv_k_k[l]), row(rwkv_k_a[l]), row(rwkv_r_k[l]), row(rwkv_lnx_g[l]), row(rwkv_lnx_b[l])]
            zp, stp = _wkv(seqs, vecs, None, n_seq=bp, seq_len=tp, row0=0, L=CHUNK, hw=1024)
            zs, sts = _wkv(seqs, vecs, _state_to_pairs(state_wkv[l].astype(F32)),
                           n_seq=bs, seq_len=ts, row0=mp, L=ts, hw=1024)
            p_wkv.append(_pairs_to_state(stp).astype(state_wkv.dtype))
            s_wkv.append(_pairs_to_state(sts).astype(state_wkv.dtype))
            z = jnp.concatenate([zp, zs], axis=0)
            x = _matmul(z, bf(rwkv_wo[l]), extras=(x,), vecs=(g0, b0), epilogue=_ln_epilogue(alpha),
                        tm=256, tn=d)
        else:
            i = l - n_a
            q = _matmul(x, attn_wq[i], out_dtype=BF16)
            table = attn_rel_bias[i]
            op = _attn_prompt(q, k_sh, v_sh, _prompt_bias(table, qb), n_seq=bp, seq_len=tp, qb=qb)
            bias_c, bias_n = _sample_bias(table, kb, ts)
            os_ = _attn_sample(q, k_sh, v_sh, cache_k.reshape(bs, kb, d), cache_v.reshape(bs, kb, d),
                               bias_c, bias_n, n_seq=bs, seq_len=ts, row0=mp)
            o = jnp.concatenate([op, os_], axis=0)
            x = _matmul(o, bf(attn_wo[i]), extras=(x,), vecs=(g0, b0), epilogue=_ln_epilogue(alpha),
                        tm=256, tn=d)
        j = l // 2
        if l % 2 == 0:
            x = _dense_ffn(x, ffn_w1[j], ffn_w3[j], ffn_w2[j], g1, b1, alpha=alpha, tm=tm_dense)
        else:
            x = _moe(x, moe_router[j], moe_w1[j], moe_w3[j], moe_w2[j], g1, b1, alpha=alpha, tm=tm_moe)
        if l == n_a - 1:
            k_sh = _matmul(x, attn_wk)
            v_sh = _matmul(x, attn_wv)

    heads = d // HEAD
    keep = min(PREV, tp)
    kp = k_sh[:mp].reshape(bp, tp, heads, HEAD)[:, tp - keep:]
    vp = v_sh[:mp].reshape(bp, tp, heads, HEAD)[:, tp - keep:]
    return (x[:mp].reshape(bp, tp, d), x[mp:].reshape(bs, ts, d),
            jnp.stack(p_wkv), jnp.stack(p_shift), kp, vp,
            jnp.stack(s_wkv), jnp.stack(s_shift),
            k_sh[mp:].reshape(bs, ts, heads, HEAD), v_sh[mp:].reshape(bs, ts, heads, HEAD))
```

```python
import functools
import math

import jax
import jax.numpy as jnp
from jax import lax
from jax.experimental import pallas as pl
from jax.experimental.pallas import tpu as pltpu

F32 = jnp.float32
BF16 = jnp.bfloat16

CHUNK = 64
PREV_CHUNKS = 8
PREV = PREV_CHUNKS * CHUNK
REL_CLIP = 128
HEAD = 64
TOP_K = 2
GN_EPS = 64e-5
LN_EPS = 1e-5
NEG_INF = -1e30

LANES = 128
SUBLANES = 8
VMEM_LIMIT_BYTES = 60 * 1024 * 1024


def _params(*sem):
    return pltpu.CompilerParams(dimension_semantics=sem, vmem_limit_bytes=VMEM_LIMIT_BYTES)


def _tile(n, target, mult=16):
    best = None
    for t in range(mult, min(n, target) + 1, mult):
        if n % t == 0:
            best = t
    assert best is not None, (n, target, mult)
    return best


def _dot(a, b):
    return jnp.dot(a, b, preferred_element_type=F32)


def _dot_nt(a, b):
    return lax.dot_general(a, b, (((1,), (1,)), ((), ())), preferred_element_type=F32)


def _dot_tn(a, b):
    return lax.dot_general(a, b, (((0,), (0,)), ((), ())), preferred_element_type=F32)


def _split_bf16(x):
    hi = x.astype(BF16)
    lo = (x - hi.astype(F32)).astype(BF16)
    return hi, lo


def _layer_norm(y, g, b):
    mu = jnp.mean(y, axis=-1, keepdims=True)
    dev = y - mu
    var = jnp.mean(dev * dev, axis=-1, keepdims=True)
    return dev * lax.rsqrt(var + LN_EPS) * g + b


def _softplus(u):
    return jnp.maximum(u, 0.0) + jnp.log(1.0 + jnp.exp(-jnp.abs(u)))


def _mm_kernel(*refs, n_extra, n_vec, epilogue, cast_w):
    x_ref, w_ref = refs[0], refs[1]
    extra = refs[2:2 + n_extra]
    vecs = refs[2 + n_extra:2 + n_extra + n_vec]
    o_ref = refs[2 + n_extra + n_vec]
    if cast_w:
        wb_ref = refs[-1]

        @pl.when(pl.program_id(1) == 0)
        def _():
            wb_ref[...] = w_ref[...].astype(BF16)

        w = wb_ref[...]
    else:
        w = w_ref[...]
    acc = _dot(x_ref[...].astype(BF16), w)
    if epilogue is not None:
        acc = epilogue(acc, *[e[...] for e in extra], *[v[...] for v in vecs])
    o_ref[...] = acc.astype(o_ref.dtype)


def _matmul(x, w, *, name, out_dtype=F32, epilogue=None, extras=(), vecs=(), tm=512, tn=1024):
    m, k = x.shape
    n = w.shape[1]
    tm = _tile(m, tm)
    tn = n if n <= tn else _tile(n, tn, LANES)
    cast_w = w.dtype != BF16
    in_specs = [pl.BlockSpec((tm, k), lambda j, i: (i, 0)),
                pl.BlockSpec((k, tn), lambda j, i: (0, j))]
    in_specs += [pl.BlockSpec((tm, tn), lambda j, i: (i, j)) for _ in extras]
    in_specs += [pl.BlockSpec((1, tn), lambda j, i: (0, j)) for _ in vecs]
    return pl.pallas_call(
        functools.partial(_mm_kernel, n_extra=len(extras), n_vec=len(vecs),
                          epilogue=epilogue, cast_w=cast_w),
        grid=(n // tn, m // tm),
        in_specs=in_specs,
        out_specs=pl.BlockSpec((tm, tn), lambda j, i: (i, j)),
        out_shape=jax.ShapeDtypeStruct((m, n), out_dtype),
        scratch_shapes=[pltpu.VMEM((k, tn), BF16)] if cast_w else [],
        compiler_params=_params("parallel", "arbitrary"),
        name=name,
    )(x, w, *extras, *vecs)


def _ln_epilogue(alpha):
    def epi(acc, xres, g, b):
        return _layer_norm(alpha * xres + acc, g, b)
    return epi


def _rwkv_pre_kernel(*refs, has_v):
    it = iter(refs)
    x_ref, halo_ref, start_ref, flag_ref, mu_ref = (next(it) for _ in range(5))
    w1, a1, g1 = next(it), next(it), next(it)
    v1 = next(it) if has_v else None
    w2, a2, g2 = next(it), next(it), next(it)
    v2 = next(it) if has_v else None
    w0, a0 = next(it), next(it)
    v0 = next(it) if has_v else None
    xr_ref, xk_ref, xv_ref, lw_ref, ag_ref, g_ref = (next(it) for _ in range(6))
    vg_ref = next(it) if has_v else None

    x = x_ref[...]
    row = lax.broadcasted_iota(jnp.int32, (x.shape[0], 1), 0)
    prev = jnp.where(row == 0, halo_ref[SUBLANES - 1:SUBLANES, :], pltpu.roll(x, 1, 0))
    flag = flag_ref[...]
    prev = jnp.where(flag == 1.0, 0.0, jnp.where(flag == 2.0, start_ref[...], prev))
    xx = prev - x
    mix = lambda i: (x + xx * mu_ref[i:i + 1, :]).astype(BF16)
    xr_ref[...] = mix(0)
    xk_ref[...] = mix(2)
    xv = mix(3)
    xv_ref[...] = xv
    hw = jnp.tanh(_dot(mix(1), w1[...]))
    z = w0[...] + _dot(hw.astype(BF16), w2[...])
    lw_ref[...] = -jnp.exp(-_softplus(-z) - 0.5)
    ha = _dot(mix(4), a1[...])
    ag_ref[...] = jax.nn.sigmoid(a0[...] + _dot(ha.astype(BF16), a2[...]))
    hg = jax.nn.sigmoid(_dot(mix(5), g1[...]))
    g_ref[...] = _dot(hg.astype(BF16), g2[...])
    if has_v:
        hv = _dot(xv, v1[...])
        vg_ref[...] = jax.nn.sigmoid(v0[...] + _dot(hv.astype(BF16), v2[...]))


def _rwkv_pre(x, start_rows, flag, mu, lora1, lora2, bias, *, tm):
    m, d = x.shape
    first_start_tile = (m - start_rows.shape[0]) // tm
    has_v = len(lora1) == 4
    row = pl.BlockSpec((tm, d), lambda i: (i, 0))
    halo = pl.BlockSpec((SUBLANES, d), lambda i: (jnp.maximum(i * (tm // SUBLANES) - 1, 0), 0))
    start = pl.BlockSpec((tm, d), lambda i: (jnp.maximum(i - first_start_tile, 0), 0))
    full = lambda a: pl.BlockSpec(a.shape, lambda i: (0,) * a.ndim)
    ins = [x, x, start_rows, flag, mu, *lora1, *lora2, *bias]
    in_specs = [row, halo, start, pl.BlockSpec((tm, 1), lambda i: (i, 0))] + [full(a) for a in ins[4:]]
    out_shape = [jax.ShapeDtypeStruct((m, d), BF16)] * 3 + [jax.ShapeDtypeStruct((m, d), F32)] * (3 + has_v)
    return pl.pallas_call(
        functools.partial(_rwkv_pre_kernel, has_v=has_v),
        grid=(m // tm,),
        in_specs=in_specs,
        out_specs=[row] * len(out_shape),
        out_shape=out_shape,
        compiler_params=_params("parallel"),
        name="rwkv_pre",
    )(*ins)


def _wkv_kernel(*refs, L, npair, nc, has_init):
    it = iter(refs)
    r_ref, k_ref, v_ref, lw_ref, ag_ref, g_ref = (next(it) for _ in range(6))
    kk_ref, ka_ref, rk_ref, lg_ref, lb_ref = (next(it) for _ in range(5))
    s0_ref = next(it) if has_init else None
    z_ref, so_ref, st_ref = next(it), next(it), next(it)
    L2 = 2 * L
    pairs = range(npair)
    bf = lambda t: t.astype(BF16)

    @pl.when(pl.program_id(2) == 0)
    def _():
        if has_init:
            zero = jnp.zeros((HEAD, HEAD), F32)
            for p in pairs:
                top = jnp.concatenate([s0_ref[0, 2 * p], zero], axis=1)
                bot = jnp.concatenate([zero, s0_ref[0, 2 * p + 1]], axis=1)
                st_ref[p] = jnp.concatenate([top, bot], axis=0).T
        else:
            st_ref[...] = jnp.zeros(st_ref.shape, F32)

    def iota(shape, dim):
        return lax.broadcasted_iota(jnp.int32, shape, dim)

    tri = (iota((L, L), 0) >= iota((L, L), 1)).astype(BF16)
    lw = lw_ref[...]
    lw_hi, lw_lo = _split_bf16(lw)
    cum = _dot(tri, lw_hi) + _dot(tri, lw_lo)
    cum_last = cum[L - 1:L, :]
    e_pos = jnp.exp(cum)
    e_neg = jnp.exp(-cum)
    e_prev = jnp.exp(cum - lw)
    e_last = jnp.exp(cum_last - cum)
    lw_split = jnp.concatenate([lw_hi, lw_lo], axis=0)

    rows, cols = iota((L2, L2), 0), iota((L2, L2), 1)
    same_head = (rows >= L) == (cols >= L)
    lower_strict = same_head & (rows > cols)
    lower_incl = same_head & (rows >= cols)
    eye = (rows == cols).astype(F32)
    own_lanes = ((iota((L2, LANES), 0) >= L) == (iota((L2, LANES), 1) >= HEAD)).astype(F32)
    ones = jnp.ones((L2, LANES), BF16)
    dup = lambda t: jnp.concatenate([t, t], axis=0)
    stack = lambda t: dup(t) * own_lanes
    unstack = lambda t: t[:L] + t[L:]
    lanes = lambda p: slice(p * LANES, (p + 1) * LANES)
    left, right = slice(0, LANES), slice(LANES, 2 * LANES)

    ops = []
    for p in pairs:
        sl = lanes(p)
        ag = ag_ref[:, sl]
        kraw = k_ref[:, sl]
        kk = stack(kraw * kk_ref[:, sl])
        kk = kk / jnp.maximum(jnp.sqrt(jnp.sum(kk * kk, axis=-1, keepdims=True)), 1e-12)
        ks = stack(kraw * (1.0 + (ag - 1.0) * ka_ref[:, sl]))
        rs = stack(r_ref[:, sl])
        vs = stack(v_ref[:, sl])
        b = kk * dup(ag)
        en = dup(e_neg[:, sl])
        el = dup(e_last[:, sl])
        ops.append(dict(
            ks=ks, rs=rs, vs=vs, vb=bf(vs),
            at=bf(-kk * dup(e_prev[:, sl])), rt=rs * dup(e_pos[:, sl]),
            bt=bf(b * en), kt=bf(ks * en), bh=bf(b * el), kh=bf(ks * el)))

    for o in ops:
        o["rtb"] = bf(o["rt"])
        if L2 % LANES == 0:
            m = _dot_nt(jnp.concatenate([o["at"], o["rtb"]], axis=0),
                        jnp.concatenate([o["bt"], o["kt"]], axis=0))
            m_ab, m_ak, m_rb, m_rk = m[:L2, :L2], m[:L2, L2:], m[L2:, :L2], m[L2:, L2:]
        else:
            m_ab, m_ak = _dot_nt(o["at"], o["bt"]), _dot_nt(o["at"], o["kt"])
            m_rb, m_rk = _dot_nt(o["rtb"], o["bt"]), _dot_nt(o["rtb"], o["kt"])
        o["m_ab"] = jnp.where(lower_strict, m_ab, 0.0)
        o["m_ak"] = bf(jnp.where(lower_strict, m_ak, 0.0))
        o["m_rb"] = bf(jnp.where(lower_incl, m_rb, 0.0))
        o["m_rk"] = bf(jnp.where(lower_incl, m_rk, 0.0))

    for o in ops:
        o["tinv"] = eye + o["m_ab"]
        o["mp"] = o["m_ab"]
    span = 2
    while span < L:
        for o in ops:
            mpb = bf(o["mp"])
            o["mp"] = _dot(mpb, mpb)
        for o in ops:
            o["tinv"] = o["tinv"] + _dot(bf(o["tinv"]), bf(o["mp"]))
        span *= 2

    for o in ops:
        o["makv"] = bf(_dot(o["m_ak"], o["vb"]))
    for o in ops:
        au = _dot(bf(o["tinv"]), jnp.concatenate([o["at"], o["makv"]], axis=1))
        o["aub"] = bf(au)
    for o in ops:
        ry = _dot(o["m_rb"], o["aub"])
        o["rh"] = bf(o["rt"] + ry[:, left])
        o["y0"] = ry[:, right] + _dot(o["m_rk"], o["vb"])
        gq = _dot_tn(o["bh"], o["aub"])
        o["g"] = bf(gq[:, left])
        o["q"] = gq[:, right] + _dot_tn(o["kh"], o["vb"])
    last = pl.program_id(2) == nc - 1
    for p, o in zip(pairs, ops):
        decay = jnp.exp(_dot_tn(lw_split[:, lanes(p)], ones))
        st = st_ref[p]
        stb = bf(st)
        o["ys"] = _dot(o["rh"], stb) + o["y0"]
        st_ref[p] = decay * st + _dot(o["g"], stb) + o["q"]

    @pl.when(last)
    def _():
        for p in pairs:
            t = st_ref[p].T
            so_ref[0, 2 * p] = t[:HEAD, :HEAD]
            so_ref[0, 2 * p + 1] = t[HEAD:, HEAD:]

    for p, o in zip(pairs, ops):
        sl = lanes(p)
        ys = o["ys"]
        mean = jnp.sum(ys, axis=-1, keepdims=True) * (1.0 / HEAD)
        dev = (ys - mean) * own_lanes
        var = jnp.sum(dev * dev, axis=-1, keepdims=True) * (1.0 / HEAD)
        yn = unstack(dev * lax.rsqrt(var + GN_EPS)) * lg_ref[:, sl] + lb_ref[:, sl]
        bonus = unstack(jnp.sum(o["rs"] * o["ks"] * rk_ref[:, sl], axis=-1, keepdims=True) * o["vs"])
        z_ref[:, sl] = ((yn + bonus) * g_ref[:, sl]).astype(z_ref.dtype)


def _wkv(seqs, vecs, s0, *, name, n_seq, seq_len, row0, L, hw):
    d = seqs[0].shape[1]
    hw = min(hw, d)
    nc = seq_len // L
    blk0 = row0 // L
    npair = hw // LANES
    seq_spec = pl.BlockSpec((L, hw), lambda b, h, c: (blk0 + b * nc + c, h))
    vec_spec = pl.BlockSpec((1, hw), lambda b, h, c: (0, h))
    st_spec = pl.BlockSpec((1, 2 * npair, HEAD, HEAD), lambda b, h, c: (b, h, 0, 0))
    in_specs = [seq_spec] * 6 + [vec_spec] * 5 + ([st_spec] if s0 is not None else [])
    ins = list(seqs) + list(vecs) + ([s0] if s0 is not None else [])
    return pl.pallas_call(
        functools.partial(_wkv_kernel, L=L, npair=npair, nc=nc, has_init=s0 is not None),
        grid=(n_seq, d // hw, nc),
        in_specs=in_specs,
        out_specs=[pl.BlockSpec((L, hw), lambda b, h, c: (b * nc + c, h)), st_spec],
        out_shape=[jax.ShapeDtypeStruct((n_seq * seq_len, d), BF16),
                   jax.ShapeDtypeStruct((n_seq, d // HEAD, HEAD, HEAD), F32)],
        scratch_shapes=[pltpu.VMEM((npair, LANES, LANES), F32)],
        compiler_params=_params("parallel", "parallel", "arbitrary"),
        name=name,
    )(*ins)


def _head_masks():
    lane = lax.broadcasted_iota(jnp.int32, (1, LANES), 1)
    return lane < HEAD


def _attn_prompt_kernel(q_ref, k0, k1, k2, v0, v1, v2, bias_ref, o_ref, *, qb):
    first = _head_masks()
    blk = pl.program_id(2)
    q = q_ref[...]
    kcat = jnp.concatenate([k0[...], k1[...], k2[...]], axis=0).astype(BF16)
    vcat = jnp.concatenate([v0[...], v1[...], v2[...]], axis=0).astype(BF16)
    kpos = lax.broadcasted_iota(jnp.int32, (1, 3 * qb), 1) + (blk - 2) * qb
    valid = kpos >= 0
    outs = []
    for h in range(2):
        qh = jnp.where(first if h == 0 else ~first, q, 0.0).astype(BF16)
        s = _dot_nt(qh, kcat) * (HEAD ** -0.5) + bias_ref[h]
        s = jnp.where(valid, s, NEG_INF)
        p = jnp.exp(s - jnp.max(s, axis=-1, keepdims=True))
        denom = jnp.sum(p, axis=-1, keepdims=True)
        outs.append(_dot(p.astype(BF16), vcat) / denom)
    o_ref[...] = jnp.where(first, outs[0], outs[1]).astype(o_ref.dtype)


def _attn_prompt(q, k, v, bias, *, n_seq, seq_len, qb):
    d = q.shape[1]
    nq = seq_len // qb
    q_spec = pl.BlockSpec((qb, LANES), lambda h, b, i: (b * nq + i, h))
    kv = lambda off: pl.BlockSpec((qb, LANES), lambda h, b, i: (b * nq + jnp.maximum(i - off, 0), h))
    return pl.pallas_call(
        functools.partial(_attn_prompt_kernel, qb=qb),
        grid=(d // LANES, n_seq, nq),
        in_specs=[q_spec, kv(2), kv(1), kv(0), kv(2), kv(1), kv(0),
                  pl.BlockSpec((2, qb, 3 * qb), lambda h, b, i: (h, 0, 0))],
        out_specs=pl.BlockSpec((qb, LANES), lambda h, b, i: (b * nq + i, h)),
        out_shape=jax.ShapeDtypeStruct((n_seq * seq_len, d), BF16),
        compiler_params=_params("parallel", "parallel", "arbitrary"),
        name="attn_prompt",
    )(q, k, k, k, v, v, v, bias)


def _attn_sample_kernel(q_ref, kn_ref, vn_ref, kc_ref, vc_ref, bc_ref, bn_ref, o_ref, *, npair):
    first = _head_masks()
    for p in range(npair):
        sl = slice(p * LANES, (p + 1) * LANES)
        q = q_ref[:, sl]
        kc, vc = kc_ref[0, :, sl].astype(BF16), vc_ref[0, :, sl].astype(BF16)
        kn, vn = kn_ref[:, sl].astype(BF16), vn_ref[:, sl].astype(BF16)
        outs = []
        for h in range(2):
            qh = jnp.where(first if h == 0 else ~first, q, 0.0).astype(BF16)
            sc = _dot_nt(qh, kc) * (HEAD ** -0.5) + bc_ref[2 * p + h]
            sn = _dot_nt(qh, kn) * (HEAD ** -0.5) + bn_ref[2 * p + h]
            mx = jnp.maximum(jnp.max(sc, axis=-1, keepdims=True), jnp.max(sn, axis=-1, keepdims=True))
            pc, pn = jnp.exp(sc - mx), jnp.exp(sn - mx)
            denom = jnp.sum(pc, axis=-1, keepdims=True) + jnp.sum(pn, axis=-1, keepdims=True)
            outs.append((_dot(pc.astype(BF16), vc) + _dot(pn.astype(BF16), vn)) / denom)
        o_ref[:, sl] = jnp.where(first, outs[0], outs[1]).astype(o_ref.dtype)


def _attn_sample(q, k, v, cache_k, cache_v, bias_c, bias_n, *, n_seq, seq_len, row0):
    d = q.shape[1]
    kb = cache_k.shape[1]
    blk0 = row0 // seq_len
    new = pl.BlockSpec((seq_len, d), lambda s: (blk0 + s, 0))
    cache = pl.BlockSpec((1, kb, d), lambda s: (s, 0, 0))
    full = lambda a: pl.BlockSpec(a.shape, lambda s: (0,) * a.ndim)
    return pl.pallas_call(
        functools.partial(_attn_sample_kernel, npair=d // LANES),
        grid=(n_seq,),
        in_specs=[new, new, new, cache, cache, full(bias_c), full(bias_n)],
        out_specs=pl.BlockSpec((seq_len, d), lambda s: (s, 0)),
        out_shape=jax.ShapeDtypeStruct((n_seq * seq_len, d), BF16),
        compiler_params=_params("parallel"),
        name="attn_sample",
    )(q, k, v, cache_k, cache_v, bias_c, bias_n)


def _prompt_bias(table, qb):
    width = 3 * qb
    dist = (width - 1) - jnp.arange(qb - 1 + width)
    by_dist = table[:, jnp.clip(dist, -REL_CLIP, REL_CLIP) + REL_CLIP].astype(F32)
    window = lambda s: lax.dynamic_slice_in_dim(by_dist, s, width, axis=1)
    bias = jax.vmap(window, out_axes=1)(qb - 1 - jnp.arange(qb))
    qi = jnp.arange(qb)[:, None]
    kj = jnp.arange(width)[None, :] - 2 * qb
    qc, kc = qi // CHUNK, jnp.floor_divide(kj, CHUNK)
    band = (kc <= qc) & (kc >= qc - PREV_CHUNKS)
    return jnp.where(band[None], bias, NEG_INF)


def _sample_bias(table, kb, t):
    dist = (kb + jnp.arange(t))[:, None] - jnp.arange(kb + t)[None, :]
    bias = table[:, jnp.clip(dist, -REL_CLIP, REL_CLIP) + REL_CLIP].astype(F32)
    return bias[:, :, :kb], bias[:, :, kb:]


def _ffn_kernel(te_ref, na_ref, x_ref, w1_ref, w3_ref, w2_ref, *rest, nf, alpha, fuse_ln):
    if fuse_ln:
        g_ref, b_ref, o_ref, xb_ref = rest
    else:
        gate_ref, o_ref = rest
    t, f = pl.program_id(0), pl.program_id(1)
    active = t < na_ref[0]

    @pl.when(f == 0)
    def _():
        o_ref[...] = jnp.zeros(o_ref.shape, F32)
        if fuse_ln:
            xb_ref[...] = x_ref[...].astype(BF16)

    @pl.when(active)
    def _():
        xb = xb_ref[...] if fuse_ln else x_ref[...]
        h1 = _dot(xb, w1_ref[0].astype(BF16))
        h3 = _dot(xb, w3_ref[0].astype(BF16))
        h = (h1 * jax.nn.sigmoid(h1) * h3).astype(BF16)
        o_ref[...] += _dot(h, w2_ref[0].astype(BF16))

    @pl.when(f == nf - 1)
    def _():
        if fuse_ln:
            o_ref[...] = _layer_norm(alpha * x_ref[...] + o_ref[...], g_ref[...], b_ref[...])
        else:
            o_ref[...] = o_ref[...] * gate_ref[...]


def _ffn(x, w1, w3, w2, tile_expert, n_active, *, name, tm, tf, gate=None, ln=None, alpha=None):
    rows, d = x.shape
    dff = w1.shape[2]
    tf = _tile(dff, tf, LANES)
    nf = dff // tf
    fuse_ln = ln is not None

    def clamp(t, na):
        return jnp.minimum(t, na[0] - 1)

    def f_idx(t, f, na):
        return jnp.where(t < na[0], f, nf - 1)

    x_spec = pl.BlockSpec((tm, d), lambda t, f, te, na: (clamp(t, na), 0))
    w13 = pl.BlockSpec((1, d, tf), lambda t, f, te, na: (te[clamp(t, na)], 0, f_idx(t, f, na)))
    w2s = pl.BlockSpec((1, tf, d), lambda t, f, te, na: (te[clamp(t, na)], f_idx(t, f, na), 0))
    o_spec = pl.BlockSpec((tm, d), lambda t, f, te, na: (t, 0))
    if fuse_ln:
        vec = pl.BlockSpec((1, d), lambda t, f, te, na: (0, 0))
        extra, extra_specs = list(ln), [vec, vec]
        scratch = [pltpu.VMEM((tm, d), BF16)]
    else:
        extra, extra_specs = [gate], [pl.BlockSpec((tm, 1), lambda t, f, te, na: (t, 0))]
        scratch = []
    return pl.pallas_call(
        functools.partial(_ffn_kernel, nf=nf, alpha=alpha, fuse_ln=fuse_ln),
        grid_spec=pltpu.PrefetchScalarGridSpec(
            num_scalar_prefetch=2,
            grid=(rows // tm, nf),
            in_specs=[x_spec, w13, w13, w2s] + extra_specs,
            out_specs=o_spec,
            scratch_shapes=scratch),
        out_shape=jax.ShapeDtypeStruct((rows, d), F32),
        compiler_params=_params("parallel", "arbitrary"),
        name=name,
    )(tile_expert, n_active, x, w1, w3, w2, *extra)


def _router_kernel(x_ref, w_ref, o_ref, *, n_exp):
    x_hi, x_lo = _split_bf16(x_ref[...])
    w_hi, w_lo = _split_bf16(w_ref[...])
    logits = _dot(x_hi, w_hi) + _dot(x_hi, w_lo) + _dot(x_lo, w_hi)
    lane = lax.broadcasted_iota(jnp.int32, logits.shape, 1).astype(F32)
    valid = lane < n_exp
    lg = jnp.where(valid, logits, NEG_INF)
    ex = jnp.where(valid, jnp.exp(lg - jnp.max(lg, axis=-1, keepdims=True)), 0.0)
    probs = jnp.where(valid, ex / jnp.sum(ex, axis=-1, keepdims=True), -1.0)
    p1 = jnp.max(probs, axis=-1, keepdims=True)
    i1 = jnp.min(jnp.where(probs == p1, lane, float(LANES)), axis=-1, keepdims=True)
    rest = jnp.where(lane == i1, -1.0, probs)
    p2 = jnp.max(rest, axis=-1, keepdims=True)
    i2 = jnp.min(jnp.where(rest == p2, lane, float(LANES)), axis=-1, keepdims=True)
    tot = p1 + p2
    out = jnp.where(lane == 0, i1,
                    jnp.where(lane == 1, i2,
                              jnp.where(lane == 2, p1 / tot, jnp.where(lane == 3, p2 / tot, 0.0))))
    o_ref[...] = out


def _router(x, w, *, tm=512):
    m, d = x.shape
    n_exp = w.shape[1]
    wp = jnp.zeros((d, LANES), F32).at[:, :n_exp].set(w)
    tm = _tile(m, tm)
    return pl.pallas_call(
        functools.partial(_router_kernel, n_exp=n_exp),
        grid=(m // tm,),
        in_specs=[pl.BlockSpec((tm, d), lambda i: (i, 0)), pl.BlockSpec((d, LANES), lambda i: (0, 0))],
        out_specs=pl.BlockSpec((tm, LANES), lambda i: (i, 0)),
        out_shape=jax.ShapeDtypeStruct((m, LANES), F32),
        compiler_params=_params("parallel"),
        name="router",
    )(x, wp)


DMA_UNROLL = 8


def _row_copy(src_ref, dst_ref, sem, src_row, dst_row):
    return pltpu.make_async_copy(src_ref.at[pl.ds(src_row, 1)], dst_ref.at[pl.ds(dst_row, 1)], sem)


def _gather_in_kernel(idx_ref, nrows_ref, x_ref, o_ref, buf, sem, *, tg):
    base = pl.program_id(0) * tg
    used = base < nrows_ref[0]

    @pl.when(used)
    def _():
        def start(i, c):
            _row_copy(x_ref, buf, sem, idx_ref[base + i], i).start()
            return c

        def wait(i, c):
            _row_copy(x_ref, buf, sem, 0, i).wait()
            return c

        lax.fori_loop(0, tg, start, 0, unroll=DMA_UNROLL)
        lax.fori_loop(0, tg, wait, 0, unroll=DMA_UNROLL)
        o_ref[...] = buf[...].astype(o_ref.dtype)

    @pl.when(jnp.logical_not(used))
    def _():
        o_ref[...] = jnp.zeros(o_ref.shape, o_ref.dtype)


def _gather_in(x, idx, n_rows, *, tg):
    d = x.shape[1]
    n = idx.shape[0]
    return pl.pallas_call(
        functools.partial(_gather_in_kernel, tg=tg),
        grid_spec=pltpu.PrefetchScalarGridSpec(
            num_scalar_prefetch=2,
            grid=(n // tg,),
            in_specs=[pl.BlockSpec(memory_space=pl.ANY)],
            out_specs=pl.BlockSpec((tg, d), lambda i, idx, nr: (i, 0)),
            scratch_shapes=[pltpu.VMEM((tg, d), F32), pltpu.SemaphoreType.DMA(())]),
        out_shape=jax.ShapeDtypeStruct((n, d), BF16),
        compiler_params=_params("arbitrary"),
        name="moe_gather",
    )(idx, n_rows, x)


def _combine_kernel(i0_ref, i1_ref, y_ref, x_ref, g_ref, b_ref, o_ref, buf, sem, *, tc, alpha):
    base = pl.program_id(0) * tc

    def start(i, c):
        _row_copy(y_ref, buf.at[0], sem, i0_ref[base + i], i).start()
        _row_copy(y_ref, buf.at[1], sem, i1_ref[base + i], i).start()
        return c

    def wait(i, c):
        _row_copy(y_ref, buf.at[0], sem, 0, i).wait()
        _row_copy(y_ref, buf.at[1], sem, 0, i).wait()
        return c

    lax.fori_loop(0, tc, start, 0, unroll=DMA_UNROLL)
    lax.fori_loop(0, tc, wait, 0, unroll=DMA_UNROLL)
    o_ref[...] = _layer_norm(alpha * x_ref[...] + (buf[0] + buf[1]), g_ref[...], b_ref[...])


def _combine(y, x, i0, i1, g, b, *, alpha, tc=256):
    m, d = x.shape
    tc = _tile(m, tc)
    row = pl.BlockSpec((tc, d), lambda i, a, c: (i, 0))
    vec = pl.BlockSpec((1, d), lambda i, a, c: (0, 0))
    return pl.pallas_call(
        functools.partial(_combine_kernel, tc=tc, alpha=alpha),
        grid_spec=pltpu.PrefetchScalarGridSpec(
            num_scalar_prefetch=2,
            grid=(m // tc,),
            in_specs=[pl.BlockSpec(memory_space=pl.ANY), row, vec, vec],
            out_specs=row,
            scratch_shapes=[pltpu.VMEM((2, tc, d), F32), pltpu.SemaphoreType.DMA(())]),
        out_shape=jax.ShapeDtypeStruct((m, d), F32),
        compiler_params=_params("arbitrary"),
        name="moe_combine",
    )(i0, i1, y, x, g, b)


def _moe(x, router, w1, w3, w2, g, b, *, alpha, tm):
    m, d = x.shape
    n_exp = w1.shape[0]
    sel = _router(x, router)
    top_i = sel[:, :TOP_K].astype(jnp.int32)
    top_p = sel[:, TOP_K:2 * TOP_K]
    e_flat = top_i.reshape(-1)
    onehot = (e_flat[:, None] == jnp.arange(n_exp)[None, :]).astype(jnp.int32)
    rank = jnp.sum((jnp.cumsum(onehot, axis=0) - onehot) * onehot, axis=1)
    counts = jnp.sum(onehot, axis=0)
    padded = (counts + tm - 1) // tm * tm
    ends = jnp.cumsum(padded)
    slot = jnp.sum(onehot * (ends - padded)[None, :], axis=1) + rank
    n_slots = (TOP_K * m + n_exp * (tm - 1)) // tm * tm
    n_tiles = n_slots // tm
    src = jnp.zeros((n_slots,), jnp.int32).at[slot].set(jnp.arange(TOP_K * m, dtype=jnp.int32) // TOP_K)
    gate = jnp.zeros((n_slots, 1), F32).at[slot, 0].set(top_p.reshape(-1))
    tile_start = jnp.arange(n_tiles, dtype=jnp.int32) * tm
    tile_expert = jnp.minimum(jnp.sum((ends[None, :] <= tile_start[:, None]).astype(jnp.int32), axis=1),
                              n_exp - 1)
    n_rows = ends[-1:].astype(jnp.int32)
    n_active = n_rows // tm

    xs = _gather_in(x, src, n_rows, tg=_tile(tm, 256))
    ys = _ffn(xs, w1, w3, w2, tile_expert, n_active, name="moe_ffn", tm=tm, tf=256, gate=gate)
    slot2 = slot.reshape(m, TOP_K).astype(jnp.int32)
    return _combine(ys, x, slot2[:, 0], slot2[:, 1], g, b, alpha=alpha)


def _dense_ffn(x, w1, w3, w2, g, b, *, alpha, tm):
    m = x.shape[0]
    n_tiles = m // tm
    return _ffn(x, w1[None].astype(BF16), w3[None].astype(BF16), w2[None].astype(BF16),
                jnp.zeros((n_tiles,), jnp.int32), jnp.full((1,), n_tiles, jnp.int32),
                name="dense_ffn", tm=tm, tf=512, ln=(g, b), alpha=alpha)


def kernel(x_prompt, x_sample, state_wkv, state_shift, cache_k, cache_v, ln_g, ln_b, rwkv_mu, rwkv_wr, rwkv_wk, rwkv_wv, rwkv_wo, rwkv_w0, rwkv_w1, rwkv_w2, rwkv_a0, rwkv_a1, rwkv_a2, rwkv_v0, rwkv_v1, rwkv_v2, rwkv_g1, rwkv_g2, rwkv_k_k, rwkv_k_a, rwkv_r_k, rwkv_lnx_g, rwkv_lnx_b, attn_wk, attn_wv, attn_wq, attn_wo, attn_rel_bias, ffn_w1, ffn_w3, ffn_w2, moe_router, moe_w1, moe_w3, moe_w2):
    bp, tp, d = x_prompt.shape
    bs, ts, _ = x_sample.shape
    depth = ln_g.shape[0]
    n_a = rwkv_wr.shape[0]
    alpha = (2 * depth) ** 0.25
    mp, ms = bp * tp, bs * ts
    m = mp + ms
    kb = cache_k.shape[1]
    assert tp % CHUNK == 0 and d % LANES == 0 and mp % ts == 0
    row = lambda a: a.reshape(1, -1)
    bf = lambda a: a.astype(BF16)
    qb = _tile(tp, 256, CHUNK)
    assert 2 * qb >= PREV
    tm_dense = _tile(m, 640)
    tm_moe = 1024 if m >= 4096 else 64
    tm_pre = _tile(math.gcd(mp, ms), 256)

    x = jnp.concatenate([x_prompt.reshape(mp, d), x_sample.reshape(ms, d)], axis=0)
    ridx = jnp.arange(m)
    start_flag = jnp.where(ridx < mp, (ridx % tp == 0) * 1.0, ((ridx - mp) % ts == 0) * 2.0)
    start_flag = start_flag.astype(F32).reshape(m, 1)

    p_wkv, s_wkv, p_shift, s_shift = [], [], [], []
    v_first = k_sh = v_sh = None
    for l in range(depth):
        g0, b0, g1, b1 = row(ln_g[l, 0]), row(ln_b[l, 0]), row(ln_g[l, 1]), row(ln_b[l, 1])
        if l < n_a:
            p_shift.append(x[:mp].reshape(bp, tp, d)[:, -1])
            s_shift.append(x[mp:].reshape(bs, ts, d)[:, -1])
            start_rows = jnp.repeat(state_shift[l], ts, axis=0)
            has_v = l > 0
            lora1 = [bf(rwkv_w1[l]), bf(rwkv_a1[l]), bf(rwkv_g1[l])] + ([bf(rwkv_v1[l - 1])] if has_v else [])
            lora2 = [bf(rwkv_w2[l]), bf(rwkv_a2[l]), bf(rwkv_g2[l])] + ([bf(rwkv_v2[l - 1])] if has_v else [])
            bias = [row(rwkv_w0[l]), row(rwkv_a0[l])] + ([row(rwkv_v0[l - 1])] if has_v else [])
            pre = _rwkv_pre(x, start_rows, start_flag, rwkv_mu[l], lora1, lora2, bias, tm=tm_pre)
            xr, xk, xv, lw, ag, gg = pre[:6]
            r = _matmul(xr, rwkv_wr[l], name="rwkv_r")
            k = _matmul(xk, rwkv_wk[l], name="rwkv_k")
            if has_v:
                v = _matmul(xv, rwkv_wv[l], name="rwkv_v", extras=(v_first, pre[6]),
                            epilogue=lambda acc, vf, vg: acc + (vf - acc) * vg)
            else:
                v = _matmul(xv, rwkv_wv[l], name="rwkv_v")
                v_first = v
            seqs = (r, k, v, lw, ag, gg)
            vecs = [row(rwkv_k_k[l]), row(rwkv_k_a[l]), row(rwkv_r_k[l]), row(rwkv_lnx_g[l]), row(rwkv_lnx_b[l])]
            zp, stp = _wkv(seqs, vecs, None, name="wkv_prompt", n_seq=bp, seq_len=tp, row0=0, L=CHUNK, hw=1024)
            zs, sts = _wkv(seqs, vecs, state_wkv[l].astype(F32), name="wkv_sample",
                           n_seq=bs, seq_len=ts, row0=mp, L=ts, hw=1024)
            p_wkv.append(stp.astype(state_wkv.dtype))
            s_wkv.append(sts.astype(state_wkv.dtype))
            z = jnp.concatenate([zp, zs], axis=0)
            x = _matmul(z, bf(rwkv_wo[l]), name="rwkv_out_ln", extras=(x,), vecs=(g0, b0),
                        epilogue=_ln_epilogue(alpha), tm=256, tn=d)
        else:
            i = l - n_a
            q = _matmul(x, attn_wq[i], name="attn_q", out_dtype=BF16)
            table = attn_rel_bias[i]
            op = _attn_prompt(q, k_sh, v_sh, _prompt_bias(table, qb), n_seq=bp, seq_len=tp, qb=qb)
            bias_c, bias_n = _sample_bias(table, kb, ts)
            os_ = _attn_sample(q, k_sh, v_sh, cache_k.reshape(bs, kb, d), cache_v.reshape(bs, kb, d),
                               bias_c, bias_n, n_seq=bs, seq_len=ts, row0=mp)
            o = jnp.concatenate([op, os_], axis=0)
            x = _matmul(o, bf(attn_wo[i]), name="attn_out_ln", extras=(x,), vecs=(g0, b0),
                        epilogue=_ln_epilogue(alpha), tm=256, tn=d)
        j = l // 2
        if l % 2 == 0:
            x = _dense_ffn(x, ffn_w1[j], ffn_w3[j], ffn_w2[j], g1, b1, alpha=alpha, tm=tm_dense)
        else:
            x = _moe(x, moe_router[j], moe_w1[j], moe_w3[j], moe_w2[j], g1, b1, alpha=alpha, tm=tm_moe)
        if l == n_a - 1:
            k_sh = _matmul(x, attn_wk, name="attn_k")
            v_sh = _matmul(x, attn_wv, name="attn_v")

    heads = d // HEAD
    keep = min(PREV, tp)
    kp = k_sh[:mp].reshape(bp, tp, heads, HEAD)[:, tp - keep:]
    vp = v_sh[:mp].reshape(bp, tp, heads, HEAD)[:, tp - keep:]
    return (x[:mp].reshape(bp, tp, d), x[mp:].reshape(bs, ts, d),
            jnp.stack(p_wkv), jnp.stack(p_shift), kp, vp,
            jnp.stack(s_wkv), jnp.stack(s_shift),
            k_sh[mp:].reshape(bs, ts, heads, HEAD), v_sh[mp:].reshape(bs, ts, heads, HEAD))
```

```python
import functools
import math

import jax
import jax.numpy as jnp
from jax import lax
from jax.experimental import pallas as pl
from jax.experimental.pallas import tpu as pltpu

F32 = jnp.float32
BF16 = jnp.bfloat16

CHUNK = 64
PREV_CHUNKS = 8
PREV = PREV_CHUNKS * CHUNK
REL_CLIP = 128
HEAD = 64
TOP_K = 2
GN_EPS = 64e-5
LN_EPS = 1e-5
NEG_INF = -1e30

LANES = 128
SUBLANES = 8
VMEM_LIMIT_BYTES = 60 * 1024 * 1024


def _params(*sem):
    return pltpu.CompilerParams(dimension_semantics=sem, vmem_limit_bytes=VMEM_LIMIT_BYTES)


def _tile(n, target, mult=16):
    best = None
    for t in range(mult, min(n, target) + 1, mult):
        if n % t == 0:
            best = t
    assert best is not None, (n, target, mult)
    return best


def _dot(a, b):
    return jnp.dot(a, b, preferred_element_type=F32)


def _dot_nt(a, b):
    return lax.dot_general(a, b, (((1,), (1,)), ((), ())), preferred_element_type=F32)


def _dot_tn(a, b):
    return lax.dot_general(a, b, (((0,), (0,)), ((), ())), preferred_element_type=F32)


def _split_bf16(x):
    hi = x.astype(BF16)
    lo = (x - hi.astype(F32)).astype(BF16)
    return hi, lo


def _layer_norm(y, g, b):
    mu = jnp.mean(y, axis=-1, keepdims=True)
    dev = y - mu
    var = jnp.mean(dev * dev, axis=-1, keepdims=True)
    return dev * lax.rsqrt(var + LN_EPS) * g + b


def _softplus(u):
    return jnp.maximum(u, 0.0) + jnp.log(1.0 + jnp.exp(-jnp.abs(u)))


def _mm_kernel(*refs, n_extra, n_vec, epilogue, cast_w):
    x_ref, w_ref = refs[0], refs[1]
    extra = refs[2:2 + n_extra]
    vecs = refs[2 + n_extra:2 + n_extra + n_vec]
    o_ref = refs[2 + n_extra + n_vec]
    if cast_w:
        wb_ref = refs[-1]

        @pl.when(pl.program_id(1) == 0)
        def _():
            wb_ref[...] = w_ref[...].astype(BF16)

        w = wb_ref[...]
    else:
        w = w_ref[...]
    acc = _dot(x_ref[...].astype(BF16), w)
    if epilogue is not None:
        acc = epilogue(acc, *[e[...] for e in extra], *[v[...] for v in vecs])
    o_ref[...] = acc.astype(o_ref.dtype)


def _matmul(x, w, *, name, out_dtype=F32, epilogue=None, extras=(), vecs=(), tm=512, tn=1024):
    m, k = x.shape
    n = w.shape[1]
    tm = _tile(m, tm)
    tn = n if n <= tn else _tile(n, tn, LANES)
    cast_w = w.dtype != BF16
    in_specs = [pl.BlockSpec((tm, k), lambda j, i: (i, 0)),
                pl.BlockSpec((k, tn), lambda j, i: (0, j))]
    in_specs += [pl.BlockSpec((tm, tn), lambda j, i: (i, j)) for _ in extras]
    in_specs += [pl.BlockSpec((1, tn), lambda j, i: (0, j)) for _ in vecs]
    return pl.pallas_call(
        functools.partial(_mm_kernel, n_extra=len(extras), n_vec=len(vecs),
                          epilogue=epilogue, cast_w=cast_w),
        grid=(n // tn, m // tm),
        in_specs=in_specs,
        out_specs=pl.BlockSpec((tm, tn), lambda j, i: (i, j)),
        out_shape=jax.ShapeDtypeStruct((m, n), out_dtype),
        scratch_shapes=[pltpu.VMEM((k, tn), BF16)] if cast_w else [],
        compiler_params=_params("parallel", "arbitrary"),
        name=name,
    )(x, w, *extras, *vecs)


def _ln_epilogue(alpha):
    def epi(acc, xres, g, b):
        return _layer_norm(alpha * xres + acc, g, b)
    return epi


def _rwkv_pre_kernel(*refs, has_v):
    it = iter(refs)
    x_ref, halo_ref, start_ref, flag_ref, mu_ref = (next(it) for _ in range(5))
    w1, a1, g1 = next(it), next(it), next(it)
    v1 = next(it) if has_v else None
    w2, a2, g2 = next(it), next(it), next(it)
    v2 = next(it) if has_v else None
    w0, a0 = next(it), next(it)
    v0 = next(it) if has_v else None
    xr_ref, xk_ref, xv_ref, lw_ref, ag_ref, g_ref = (next(it) for _ in range(6))
    vg_ref = next(it) if has_v else None

    x = x_ref[...]
    row = lax.broadcasted_iota(jnp.int32, (x.shape[0], 1), 0)
    prev = jnp.where(row == 0, halo_ref[SUBLANES - 1:SUBLANES, :], pltpu.roll(x, 1, 0))
    flag = flag_ref[...]
    prev = jnp.where(flag == 1.0, 0.0, jnp.where(flag == 2.0, start_ref[...], prev))
    xx = prev - x
    mix = lambda i: (x + xx * mu_ref[i:i + 1, :]).astype(BF16)
    xr_ref[...] = mix(0)
    xk_ref[...] = mix(2)
    xv = mix(3)
    xv_ref[...] = xv
    hw = jnp.tanh(_dot(mix(1), w1[...]))
    z = w0[...] + _dot(hw.astype(BF16), w2[...])
    lw_ref[...] = -jnp.exp(-_softplus(-z) - 0.5)
    ha = _dot(mix(4), a1[...])
    ag_ref[...] = jax.nn.sigmoid(a0[...] + _dot(ha.astype(BF16), a2[...]))
    hg = jax.nn.sigmoid(_dot(mix(5), g1[...]))
    g_ref[...] = _dot(hg.astype(BF16), g2[...])
    if has_v:
        hv = _dot(xv, v1[...])
        vg_ref[...] = jax.nn.sigmoid(v0[...] + _dot(hv.astype(BF16), v2[...]))


def _rwkv_pre(x, start_rows, flag, mu, lora1, lora2, bias, *, tm):
    m, d = x.shape
    first_start_tile = (m - start_rows.shape[0]) // tm
    has_v = len(lora1) == 4
    row = pl.BlockSpec((tm, d), lambda i: (i, 0))
    halo = pl.BlockSpec((SUBLANES, d), lambda i: (jnp.maximum(i * (tm // SUBLANES) - 1, 0), 0))
    start = pl.BlockSpec((tm, d), lambda i: (jnp.maximum(i - first_start_tile, 0), 0))
    full = lambda a: pl.BlockSpec(a.shape, lambda i: (0,) * a.ndim)
    ins = [x, x, start_rows, flag, mu, *lora1, *lora2, *bias]
    in_specs = [row, halo, start, pl.BlockSpec((tm, 1), lambda i: (i, 0))] + [full(a) for a in ins[4:]]
    out_shape = [jax.ShapeDtypeStruct((m, d), BF16)] * 3 + [jax.ShapeDtypeStruct((m, d), F32)] * (3 + has_v)
    return pl.pallas_call(
        functools.partial(_rwkv_pre_kernel, has_v=has_v),
        grid=(m // tm,),
        in_specs=in_specs,
        out_specs=[row] * len(out_shape),
        out_shape=out_shape,
        compiler_params=_params("parallel"),
        name="rwkv_pre",
    )(*ins)


def _wkv_kernel(*refs, L, npair, nc, has_init):
    it = iter(refs)
    r_ref, k_ref, v_ref, lw_ref, ag_ref, g_ref = (next(it) for _ in range(6))
    kk_ref, ka_ref, rk_ref, lg_ref, lb_ref = (next(it) for _ in range(5))
    s0_ref = next(it) if has_init else None
    z_ref, so_ref, st_ref = next(it), next(it), next(it)
    L2 = 2 * L
    pairs = range(npair)
    bf = lambda t: t.astype(BF16)

    @pl.when(pl.program_id(2) == 0)
    def _():
        if has_init:
            zero = jnp.zeros((HEAD, HEAD), F32)
            for p in pairs:
                top = jnp.concatenate([s0_ref[0, 2 * p], zero], axis=1)
                bot = jnp.concatenate([zero, s0_ref[0, 2 * p + 1]], axis=1)
                st_ref[p] = jnp.concatenate([top, bot], axis=0).T
        else:
            st_ref[...] = jnp.zeros(st_ref.shape, F32)

    def iota(shape, dim):
        return lax.broadcasted_iota(jnp.int32, shape, dim)

    tri = (iota((L, L), 0) >= iota((L, L), 1)).astype(BF16)
    lw = lw_ref[...]
    lw_hi, lw_lo = _split_bf16(lw)
    cum = _dot(tri, lw_hi) + _dot(tri, lw_lo)
    cum_last = cum[L - 1:L, :]
    e_pos = jnp.exp(cum)
    e_neg = jnp.exp(-cum)
    e_prev = jnp.exp(cum - lw)
    e_last = jnp.exp(cum_last - cum)
    lw_split = jnp.concatenate([lw_hi, lw_lo], axis=0)

    rows, cols = iota((L2, L2), 0), iota((L2, L2), 1)
    same_head = (rows >= L) == (cols >= L)
    lower_strict = same_head & (rows > cols)
    lower_incl = same_head & (rows >= cols)
    eye = (rows == cols).astype(F32)
    own_lanes = ((iota((L2, LANES), 0) >= L) == (iota((L2, LANES), 1) >= HEAD)).astype(F32)
    ones = jnp.ones((L2, LANES), BF16)
    dup = lambda t: jnp.concatenate([t, t], axis=0)
    stack = lambda t: dup(t) * own_lanes
    unstack = lambda t: t[:L] + t[L:]
    lanes = lambda p: slice(p * LANES, (p + 1) * LANES)
    left, right = slice(0, LANES), slice(LANES, 2 * LANES)

    ops = []
    for p in pairs:
        sl = lanes(p)
        ag = ag_ref[:, sl]
        kraw = k_ref[:, sl]
        kk = stack(kraw * kk_ref[:, sl])
        kk = kk / jnp.maximum(jnp.sqrt(jnp.sum(kk * kk, axis=-1, keepdims=True)), 1e-12)
        ks = stack(kraw * (1.0 + (ag - 1.0) * ka_ref[:, sl]))
        rs = stack(r_ref[:, sl])
        vs = stack(v_ref[:, sl])
        b = kk * dup(ag)
        en = dup(e_neg[:, sl])
        el = dup(e_last[:, sl])
        ops.append(dict(
            ks=ks, rs=rs, vs=vs, vb=bf(vs),
            at=bf(-kk * dup(e_prev[:, sl])), rt=rs * dup(e_pos[:, sl]),
            bt=bf(b * en), kt=bf(ks * en), bh=bf(b * el), kh=bf(ks * el)))

    for o in ops:
        o["rtb"] = bf(o["rt"])
        if L2 % LANES == 0:
            m = _dot_nt(jnp.concatenate([o["at"], o["rtb"]], axis=0),
                        jnp.concatenate([o["bt"], o["kt"]], axis=0))
            m_ab, m_ak, m_rb, m_rk = m[:L2, :L2], m[:L2, L2:], m[L2:, :L2], m[L2:, L2:]
        else:
            m_ab, m_ak = _dot_nt(o["at"], o["bt"]), _dot_nt(o["at"], o["kt"])
            m_rb, m_rk = _dot_nt(o["rtb"], o["bt"]), _dot_nt(o["rtb"], o["kt"])
        o["m_ab"] = jnp.where(lower_strict, m_ab, 0.0)
        o["m_ak"] = bf(jnp.where(lower_strict, m_ak, 0.0))
        o["m_rb"] = bf(jnp.where(lower_incl, m_rb, 0.0))
        o["m_rk"] = bf(jnp.where(lower_incl, m_rk, 0.0))

    for o in ops:
        o["tinv"] = eye + o["m_ab"]
        o["mp"] = o["m_ab"]
    span = 2
    while span < L:
        for o in ops:
            mpb = bf(o["mp"])
            o["mp"] = _dot(mpb, mpb)
        for o in ops:
            o["tinv"] = o["tinv"] + _dot(bf(o["tinv"]), bf(o["mp"]))
        span *= 2

    for o in ops:
        o["makv"] = bf(_dot(o["m_ak"], o["vb"]))
    for o in ops:
        au = _dot(bf(o["tinv"]), jnp.concatenate([o["at"], o["makv"]], axis=1))
        o["aub"] = bf(au)
    for o in ops:
        ry = _dot(o["m_rb"], o["aub"])
        o["rh"] = bf(o["rt"] + ry[:, left])
        o["y0"] = ry[:, right] + _dot(o["m_rk"], o["vb"])
        gq = _dot_tn(o["bh"], o["aub"])
        o["g"] = bf(gq[:, left])
        o["q"] = gq[:, right] + _dot_tn(o["kh"], o["vb"])
    last = pl.program_id(2) == nc - 1
    for p, o in zip(pairs, ops):
        decay = jnp.exp(_dot_tn(lw_split[:, lanes(p)], ones))
        st = st_ref[p]
        stb = bf(st)
        o["ys"] = _dot(o["rh"], stb) + o["y0"]
        st_ref[p] = decay * st + _dot(o["g"], stb) + o["q"]

    @pl.when(last)
    def _():
        for p in pairs:
            t = st_ref[p].T
            so_ref[0, 2 * p] = t[:HEAD, :HEAD]
            so_ref[0, 2 * p + 1] = t[HEAD:, HEAD:]

    for p, o in zip(pairs, ops):
        sl = lanes(p)
        ys = o["ys"]
        mean = jnp.sum(ys, axis=-1, keepdims=True) * (1.0 / HEAD)
        dev = (ys - mean) * own_lanes
        var = jnp.sum(dev * dev, axis=-1, keepdims=True) * (1.0 / HEAD)
        yn = unstack(dev * lax.rsqrt(var + GN_EPS)) * lg_ref[:, sl] + lb_ref[:, sl]
        bonus = unstack(jnp.sum(o["rs"] * o["ks"] * rk_ref[:, sl], axis=-1, keepdims=True) * o["vs"])
        z_ref[:, sl] = ((yn + bonus) * g_ref[:, sl]).astype(z_ref.dtype)


def _wkv(seqs, vecs, s0, *, name, n_seq, seq_len, row0, L, hw):
    d = seqs[0].shape[1]
    hw = min(hw, d)
    nc = seq_len // L
    blk0 = row0 // L
    npair = hw // LANES
    seq_spec = pl.BlockSpec((L, hw), lambda b, h, c: (blk0 + b * nc + c, h))
    vec_spec = pl.BlockSpec((1, hw), lambda b, h, c: (0, h))
    st_spec = pl.BlockSpec((1, 2 * npair, HEAD, HEAD), lambda b, h, c: (b, h, 0, 0))
    in_specs = [seq_spec] * 6 + [vec_spec] * 5 + ([st_spec] if s0 is not None else [])
    ins = list(seqs) + list(vecs) + ([s0] if s0 is not None else [])
    return pl.pallas_call(
        functools.partial(_wkv_kernel, L=L, npair=npair, nc=nc, has_init=s0 is not None),
        grid=(n_seq, d // hw, nc),
        in_specs=in_specs,
        out_specs=[pl.BlockSpec((L, hw), lambda b, h, c: (b * nc + c, h)), st_spec],
        out_shape=[jax.ShapeDtypeStruct((n_seq * seq_len, d), BF16),
                   jax.ShapeDtypeStruct((n_seq, d // HEAD, HEAD, HEAD), F32)],
        scratch_shapes=[pltpu.VMEM((npair, LANES, LANES), F32)],
        compiler_params=_params("parallel", "parallel", "arbitrary"),
        name=name,
    )(*ins)


def _head_masks():
    lane = lax.broadcasted_iota(jnp.int32, (1, LANES), 1)
    return lane < HEAD


def _attn_prompt_kernel(q_ref, k0, k1, k2, v0, v1, v2, bias_ref, o_ref, *, qb):
    first = _head_masks()
    blk = pl.program_id(2)
    q = q_ref[...]
    kcat = jnp.concatenate([k0[...], k1[...], k2[...]], axis=0).astype(BF16)
    vcat = jnp.concatenate([v0[...], v1[...], v2[...]], axis=0).astype(BF16)
    kpos = lax.broadcasted_iota(jnp.int32, (1, 3 * qb), 1) + (blk - 2) * qb
    valid = kpos >= 0
    outs = []
    for h in range(2):
        qh = jnp.where(first if h == 0 else ~first, q, 0.0).astype(BF16)
        s = _dot_nt(qh, kcat) * (HEAD ** -0.5) + bias_ref[h]
        s = jnp.where(valid, s, NEG_INF)
        p = jnp.exp(s - jnp.max(s, axis=-1, keepdims=True))
        denom = jnp.sum(p, axis=-1, keepdims=True)
        outs.append(_dot(p.astype(BF16), vcat) / denom)
    o_ref[...] = jnp.where(first, outs[0], outs[1]).astype(o_ref.dtype)


def _attn_prompt(q, k, v, bias, *, n_seq, seq_len, qb):
    d = q.shape[1]
    nq = seq_len // qb
    q_spec = pl.BlockSpec((qb, LANES), lambda h, b, i: (b * nq + i, h))
    kv = lambda off: pl.BlockSpec((qb, LANES), lambda h, b, i: (b * nq + jnp.maximum(i - off, 0), h))
    return pl.pallas_call(
        functools.partial(_attn_prompt_kernel, qb=qb),
        grid=(d // LANES, n_seq, nq),
        in_specs=[q_spec, kv(2), kv(1), kv(0), kv(2), kv(1), kv(0),
                  pl.BlockSpec((2, qb, 3 * qb), lambda h, b, i: (h, 0, 0))],
        out_specs=pl.BlockSpec((qb, LANES), lambda h, b, i: (b * nq + i, h)),
        out_shape=jax.ShapeDtypeStruct((n_seq * seq_len, d), BF16),
        compiler_params=_params("parallel", "parallel", "arbitrary"),
        name="attn_prompt",
    )(q, k, k, k, v, v, v, bias)


def _attn_sample_kernel(q_ref, kn_ref, vn_ref, kc_ref, vc_ref, bc_ref, bn_ref, o_ref, *, npair):
    first = _head_masks()
    lanes = lambda p: slice(p * LANES, (p + 1) * LANES)
    units = []
    for p in range(npair):
        q = q_ref[:, lanes(p)]
        kc, kn = kc_ref[0, :, lanes(p)].astype(BF16), kn_ref[:, lanes(p)].astype(BF16)
        for h in range(2):
            qh = jnp.where(first if h == 0 else ~first, q, 0.0).astype(BF16)
            units.append(dict(p=p, h=2 * p + h, sc=_dot_nt(qh, kc), sn=_dot_nt(qh, kn)))
    for u in units:
        sc = u["sc"] * (HEAD ** -0.5) + bc_ref[u["h"]]
        sn = u["sn"] * (HEAD ** -0.5) + bn_ref[u["h"]]
        mx = jnp.maximum(jnp.max(sc, axis=-1, keepdims=True), jnp.max(sn, axis=-1, keepdims=True))
        pc, pn = jnp.exp(sc - mx), jnp.exp(sn - mx)
        u["denom"] = jnp.sum(pc, axis=-1, keepdims=True) + jnp.sum(pn, axis=-1, keepdims=True)
        u["pc"], u["pn"] = pc.astype(BF16), pn.astype(BF16)
    for u in units:
        vc, vn = vc_ref[0, :, lanes(u["p"])].astype(BF16), vn_ref[:, lanes(u["p"])].astype(BF16)
        u["o"] = (_dot(u["pc"], vc) + _dot(u["pn"], vn)) / u["denom"]
    for p in range(npair):
        o_ref[:, lanes(p)] = jnp.where(first, units[2 * p]["o"], units[2 * p + 1]["o"]).astype(o_ref.dtype)


def _attn_sample(q, k, v, cache_k, cache_v, bias_c, bias_n, *, n_seq, seq_len, row0):
    d = q.shape[1]
    kb = cache_k.shape[1]
    blk0 = row0 // seq_len
    new = pl.BlockSpec((seq_len, d), lambda s: (blk0 + s, 0))
    cache = pl.BlockSpec((1, kb, d), lambda s: (s, 0, 0))
    full = lambda a: pl.BlockSpec(a.shape, lambda s: (0,) * a.ndim)
    return pl.pallas_call(
        functools.partial(_attn_sample_kernel, npair=d // LANES),
        grid=(n_seq,),
        in_specs=[new, new, new, cache, cache, full(bias_c), full(bias_n)],
        out_specs=pl.BlockSpec((seq_len, d), lambda s: (s, 0)),
        out_shape=jax.ShapeDtypeStruct((n_seq * seq_len, d), BF16),
        compiler_params=_params("parallel"),
        name="attn_sample",
    )(q, k, v, cache_k, cache_v, bias_c, bias_n)


def _to_heads_kernel(*refs):
    n = len(refs) // 2
    for x_ref, o_ref in zip(refs[:n], refs[n:]):
        o_ref[...] = x_ref[...].reshape(o_ref.shape)


def _to_heads(arrs, *, row0, seq_stride, n_seq, rows):
    d = arrs[0].shape[1]
    tb = _tile(math.gcd(math.gcd(row0, seq_stride), rows), LANES, SUBLANES)
    in_spec = pl.BlockSpec((tb, d), lambda s, i: ((row0 + s * seq_stride) // tb + i, 0))
    out_spec = pl.BlockSpec((tb, d // HEAD, HEAD), lambda s, i: (s * (rows // tb) + i, 0, 0))
    return pl.pallas_call(
        _to_heads_kernel,
        grid=(n_seq, rows // tb),
        in_specs=[in_spec] * len(arrs),
        out_specs=[out_spec] * len(arrs),
        out_shape=[jax.ShapeDtypeStruct((n_seq * rows, d // HEAD, HEAD), a.dtype) for a in arrs],
        compiler_params=_params("parallel", "parallel"),
        name="to_heads",
    )(*arrs)


def _prompt_bias(table, qb):
    width = 3 * qb
    period = width + qb
    dist = (2 * qb - jnp.arange(period)) % period
    dist = jnp.where(dist > width, dist - period, dist)
    by_dist = table[:, jnp.clip(dist, -REL_CLIP, REL_CLIP) + REL_CLIP].astype(F32)
    heads = table.shape[0]
    rolled = jnp.tile(by_dist, (1, qb))[:, :qb * (period - 1)].reshape(heads, qb, period - 1)
    bias = rolled[:, :, :width]
    qi = jnp.arange(qb)[:, None]
    kj = jnp.arange(width)[None, :] - 2 * qb
    qc, kc = qi // CHUNK, jnp.floor_divide(kj, CHUNK)
    band = (kc <= qc) & (kc >= qc - PREV_CHUNKS)
    return jnp.where(band[None], bias, NEG_INF)


def _sample_bias(table, kb, t):
    dist = (kb + jnp.arange(t))[:, None] - jnp.arange(kb + t)[None, :]
    bias = table[:, jnp.clip(dist, -REL_CLIP, REL_CLIP) + REL_CLIP].astype(F32)
    return bias[:, :, :kb], bias[:, :, kb:]


def _swiglu_step(xb, w1, w3, w2):
    h1 = _dot(xb, w1)
    h3 = _dot(xb, w3)
    return _dot((h1 * jax.nn.sigmoid(h1) * h3).astype(BF16), w2)


def _dense_ffn_kernel(x_ref, w1_ref, w3_ref, w2_ref, g_ref, b_ref, o_ref, xb_ref, *, nf, alpha):
    f = pl.program_id(1)

    @pl.when(f == 0)
    def _():
        o_ref[...] = jnp.zeros(o_ref.shape, F32)
        xb_ref[...] = x_ref[...].astype(BF16)

    o_ref[...] += _swiglu_step(xb_ref[...], w1_ref[...].astype(BF16), w3_ref[...].astype(BF16),
                               w2_ref[...].astype(BF16))

    @pl.when(f == nf - 1)
    def _():
        o_ref[...] = _layer_norm(alpha * x_ref[...] + o_ref[...], g_ref[...], b_ref[...])


def _dense_ffn(x, w1, w3, w2, g, b, *, alpha, tm, tf=512):
    rows, d = x.shape
    dff = w1.shape[1]
    tf = _tile(dff, tf, LANES)
    nf = dff // tf
    row = pl.BlockSpec((tm, d), lambda t, f: (t, 0))
    vec = pl.BlockSpec((1, d), lambda t, f: (0, 0))
    w13 = pl.BlockSpec((d, tf), lambda t, f: (0, f))
    return pl.pallas_call(
        functools.partial(_dense_ffn_kernel, nf=nf, alpha=alpha),
        grid=(rows // tm, nf),
        in_specs=[row, w13, w13, pl.BlockSpec((tf, d), lambda t, f: (f, 0)), vec, vec],
        out_specs=row,
        out_shape=jax.ShapeDtypeStruct((rows, d), F32),
        scratch_shapes=[pltpu.VMEM((tm, d), BF16)],
        compiler_params=_params("parallel", "arbitrary"),
        name="dense_ffn",
    )(x, w1.astype(BF16), w3.astype(BF16), w2.astype(BF16), g, b)


def _router_kernel(x_ref, w_ref, o_ref, *, n_exp):
    x_hi, x_lo = _split_bf16(x_ref[...])
    w_hi, w_lo = _split_bf16(w_ref[...])
    logits = _dot(x_hi, w_hi) + _dot(x_hi, w_lo) + _dot(x_lo, w_hi)
    lane = lax.broadcasted_iota(jnp.int32, logits.shape, 1).astype(F32)
    valid = lane < n_exp
    lg = jnp.where(valid, logits, NEG_INF)
    ex = jnp.where(valid, jnp.exp(lg - jnp.max(lg, axis=-1, keepdims=True)), 0.0)
    probs = jnp.where(valid, ex / jnp.sum(ex, axis=-1, keepdims=True), -1.0)
    p1 = jnp.max(probs, axis=-1, keepdims=True)
    i1 = jnp.min(jnp.where(probs == p1, lane, float(LANES)), axis=-1, keepdims=True)
    rest = jnp.where(lane == i1, -1.0, probs)
    p2 = jnp.max(rest, axis=-1, keepdims=True)
    i2 = jnp.min(jnp.where(rest == p2, lane, float(LANES)), axis=-1, keepdims=True)
    tot = p1 + p2
    out = jnp.where(lane == 0, i1,
                    jnp.where(lane == 1, i2,
                              jnp.where(lane == 2, p1 / tot, jnp.where(lane == 3, p2 / tot, 0.0))))
    o_ref[...] = out


def _router(x, w, *, tm=512):
    m, d = x.shape
    n_exp = w.shape[1]
    wp = jnp.zeros((d, LANES), F32).at[:, :n_exp].set(w)
    tm = _tile(m, tm)
    return pl.pallas_call(
        functools.partial(_router_kernel, n_exp=n_exp),
        grid=(m // tm,),
        in_specs=[pl.BlockSpec((tm, d), lambda i: (i, 0)), pl.BlockSpec((d, LANES), lambda i: (0, 0))],
        out_specs=pl.BlockSpec((tm, LANES), lambda i: (i, 0)),
        out_shape=jax.ShapeDtypeStruct((m, LANES), F32),
        compiler_params=_params("parallel"),
        name="router",
    )(x, wp)


DMA_UNROLL = 8


def _row_copy(src_ref, dst_ref, sem, src_row, dst_row):
    return pltpu.make_async_copy(src_ref.at[pl.ds(src_row, 1)], dst_ref.at[pl.ds(dst_row, 1)], sem)


def _moe_ffn_kernel(te_ref, rows_ref, na_ref, src_ref, x_hbm, w1_ref, w3_ref, w2_ref, o_ref,
                    gbuf, xb_ref, sem, *, n_tiles, tm, sub, n_issue):
    t, f = pl.program_id(0), pl.program_id(1)
    rows = rows_ref[t]
    chunk = tm // n_issue

    def issue(tile, lo, n):
        def body(i, c):
            _row_copy(x_hbm, gbuf, sem, src_ref[tile * tm + lo + i], lo + i).start()
            return c
        lax.fori_loop(0, n, body, 0, unroll=DMA_UNROLL)

    def wait_tile():
        def body(i, c):
            _row_copy(x_hbm, gbuf, sem, 0, i).wait()
            return c
        lax.fori_loop(0, tm, body, 0, unroll=DMA_UNROLL)

    @pl.when(jnp.logical_and(jnp.logical_and(t == 0, f == 0), rows > 0))
    def _():
        issue(0, 0, tm)

    @pl.when(f == 0)
    def _():
        o_ref[...] = jnp.zeros(o_ref.shape, F32)

        @pl.when(rows > 0)
        def _():
            wait_tile()
            xb_ref[...] = gbuf[...].astype(BF16)

    nxt = jnp.minimum(t + 1, n_tiles - 1)
    fetch_next = jnp.logical_and(jnp.logical_and(f >= 1, f <= n_issue),
                                 jnp.logical_and(t + 1 < n_tiles, rows_ref[nxt] > 0))

    @pl.when(fetch_next)
    def _():
        issue(t + 1, (f - 1) * chunk, chunk)

    @pl.when(rows > 0)
    def _():
        w1, w3, w2 = w1_ref[0].astype(BF16), w3_ref[0].astype(BF16), w2_ref[0].astype(BF16)
        for s in range(tm // sub):
            @pl.when(s * sub < rows)
            def _():
                blk = pl.ds(s * sub, sub)
                o_ref[blk, :] += _swiglu_step(xb_ref[blk, :], w1, w3, w2)


def _moe_ffn(x, src, w1, w3, w2, tile_expert, tile_rows, n_active, *, tm, tf):
    d = x.shape[1]
    n_slots = src.shape[0]
    n_tiles = n_slots // tm
    dff = w1.shape[2]
    tf = _tile(dff, tf, LANES)
    nf = dff // tf
    n_issue = 1
    while 2 * n_issue <= max(nf - 1, 1) and tm % (2 * n_issue) == 0:
        n_issue *= 2
    assert nf >= 2
    sub = min(tm, 256)

    def clamp(t, na):
        return jnp.minimum(t, na[0] - 1)

    def f_idx(t, f, na):
        return jnp.where(t < na[0], f, nf - 1)

    w13 = pl.BlockSpec((1, d, tf), lambda t, f, te, tr, na, sr: (te[clamp(t, na)], 0, f_idx(t, f, na)))
    w2s = pl.BlockSpec((1, tf, d), lambda t, f, te, tr, na, sr: (te[clamp(t, na)], f_idx(t, f, na), 0))
    return pl.pallas_call(
        functools.partial(_moe_ffn_kernel, n_tiles=n_tiles, tm=tm, sub=sub, n_issue=n_issue),
        grid_spec=pltpu.PrefetchScalarGridSpec(
            num_scalar_prefetch=4,
            grid=(n_tiles, nf),
            in_specs=[pl.BlockSpec(memory_space=pl.ANY), w13, w13, w2s],
            out_specs=pl.BlockSpec((tm, d), lambda t, f, te, tr, na, sr: (t, 0)),
            scratch_shapes=[pltpu.VMEM((tm, d), F32), pltpu.VMEM((tm, d), BF16),
                            pltpu.SemaphoreType.DMA(())]),
        out_shape=jax.ShapeDtypeStruct((n_slots, d), F32),
        compiler_params=_params("arbitrary", "arbitrary"),
        name="moe_ffn",
    )(tile_expert, tile_rows, n_active, src, x, w1, w3, w2)


def _combine_kernel(i0_ref, i1_ref, y_ref, x_ref, sel_ref, g_ref, b_ref, o_ref, buf, sem, *, tc, alpha):
    base = pl.program_id(0) * tc

    def start(i, c):
        _row_copy(y_ref, buf.at[0], sem, i0_ref[base + i], i).start()
        _row_copy(y_ref, buf.at[1], sem, i1_ref[base + i], i).start()
        return c

    def wait(i, c):
        _row_copy(y_ref, buf.at[0], sem, 0, i).wait()
        _row_copy(y_ref, buf.at[1], sem, 0, i).wait()
        return c

    lax.fori_loop(0, tc, start, 0, unroll=DMA_UNROLL)
    lax.fori_loop(0, tc, wait, 0, unroll=DMA_UNROLL)
    sel = sel_ref[...]
    mixed = buf[0] * sel[:, TOP_K:TOP_K + 1] + buf[1] * sel[:, TOP_K + 1:TOP_K + 2]
    o_ref[...] = _layer_norm(alpha * x_ref[...] + mixed, g_ref[...], b_ref[...])


def _combine(y, x, sel, i0, i1, g, b, *, alpha, tc=256):
    m, d = x.shape
    tc = _tile(m, tc)
    row = pl.BlockSpec((tc, d), lambda i, a, c: (i, 0))
    vec = pl.BlockSpec((1, d), lambda i, a, c: (0, 0))
    return pl.pallas_call(
        functools.partial(_combine_kernel, tc=tc, alpha=alpha),
        grid_spec=pltpu.PrefetchScalarGridSpec(
            num_scalar_prefetch=2,
            grid=(m // tc,),
            in_specs=[pl.BlockSpec(memory_space=pl.ANY), row,
                      pl.BlockSpec((tc, LANES), lambda i, a, c: (i, 0)), vec, vec],
            out_specs=row,
            scratch_shapes=[pltpu.VMEM((2, tc, d), F32), pltpu.SemaphoreType.DMA(())]),
        out_shape=jax.ShapeDtypeStruct((m, d), F32),
        compiler_params=_params("arbitrary"),
        name="moe_combine",
    )(i0, i1, y, x, sel, g, b)


def _moe(x, router, w1, w3, w2, g, b, *, alpha, tm):
    m, d = x.shape
    n_exp = w1.shape[0]
    sel = _router(x, router)
    top_i = sel[:, :TOP_K].astype(jnp.int32)
    e_flat = top_i.reshape(-1)
    onehot = (e_flat[:, None] == jnp.arange(n_exp)[None, :]).astype(jnp.int32)
    rank = jnp.sum((jnp.cumsum(onehot, axis=0) - onehot) * onehot, axis=1)
    counts = jnp.sum(onehot, axis=0)
    padded = (counts + tm - 1) // tm * tm
    ends = jnp.cumsum(padded)
    slot = jnp.sum(onehot * (ends - padded)[None, :], axis=1) + rank
    n_slots = (TOP_K * m + n_exp * (tm - 1)) // tm * tm
    n_tiles = n_slots // tm
    src = jnp.zeros((n_slots,), jnp.int32).at[slot].set(jnp.arange(TOP_K * m, dtype=jnp.int32) // TOP_K)
    tile_start = jnp.arange(n_tiles, dtype=jnp.int32) * tm
    tile_expert = jnp.minimum(jnp.sum((ends[None, :] <= tile_start[:, None]).astype(jnp.int32), axis=1),
                              n_exp - 1).astype(jnp.int32)
    group_fill = (ends - padded + counts).astype(jnp.int32)
    tile_rows = jnp.clip(group_fill[tile_expert] - tile_start, 0, tm).astype(jnp.int32)
    n_active = (ends[-1:] // tm).astype(jnp.int32)

    ys = _moe_ffn(x, src, w1, w3, w2, tile_expert, tile_rows, n_active, tm=tm, tf=256)
    slot2 = slot.reshape(m, TOP_K).astype(jnp.int32)
    return _combine(ys, x, sel, slot2[:, 0], slot2[:, 1], g, b, alpha=alpha)


def kernel(x_prompt, x_sample, state_wkv, state_shift, cache_k, cache_v, ln_g, ln_b, rwkv_mu, rwkv_wr, rwkv_wk, rwkv_wv, rwkv_wo, rwkv_w0, rwkv_w1, rwkv_w2, rwkv_a0, rwkv_a1, rwkv_a2, rwkv_v0, rwkv_v1, rwkv_v2, rwkv_g1, rwkv_g2, rwkv_k_k, rwkv_k_a, rwkv_r_k, rwkv_lnx_g, rwkv_lnx_b, attn_wk, attn_wv, attn_wq, attn_wo, attn_rel_bias, ffn_w1, ffn_w3, ffn_w2, moe_router, moe_w1, moe_w3, moe_w2):
    bp, tp, d = x_prompt.shape
    bs, ts, _ = x_sample.shape
    depth = ln_g.shape[0]
    n_a = rwkv_wr.shape[0]
    alpha = (2 * depth) ** 0.25
    mp, ms = bp * tp, bs * ts
    m = mp + ms
    kb = cache_k.shape[1]
    assert tp % CHUNK == 0 and d % LANES == 0 and mp % ts == 0
    row = lambda a: a.reshape(1, -1)
    bf = lambda a: a.astype(BF16)
    qb = _tile(tp, 256, CHUNK)
    assert 2 * qb >= PREV
    tm_dense = _tile(m, 640)
    tm_moe = 1024 if m >= 4096 else 64
    tm_pre = _tile(math.gcd(mp, ms), 256)

    x = jnp.concatenate([x_prompt.reshape(mp, d), x_sample.reshape(ms, d)], axis=0)
    ridx = jnp.arange(m)
    start_flag = jnp.where(ridx < mp, (ridx % tp == 0) * 1.0, ((ridx - mp) % ts == 0) * 2.0)
    start_flag = start_flag.astype(F32).reshape(m, 1)

    p_wkv, s_wkv, p_shift, s_shift = [], [], [], []
    v_first = k_sh = v_sh = None
    for l in range(depth):
        g0, b0, g1, b1 = row(ln_g[l, 0]), row(ln_b[l, 0]), row(ln_g[l, 1]), row(ln_b[l, 1])
        if l < n_a:
            p_shift.append(x[:mp].reshape(bp, tp, d)[:, -1])
            s_shift.append(x[mp:].reshape(bs, ts, d)[:, -1])
            start_rows = jnp.repeat(state_shift[l], ts, axis=0)
            has_v = l > 0
            lora1 = [bf(rwkv_w1[l]), bf(rwkv_a1[l]), bf(rwkv_g1[l])] + ([bf(rwkv_v1[l - 1])] if has_v else [])
            lora2 = [bf(rwkv_w2[l]), bf(rwkv_a2[l]), bf(rwkv_g2[l])] + ([bf(rwkv_v2[l - 1])] if has_v else [])
            bias = [row(rwkv_w0[l]), row(rwkv_a0[l])] + ([row(rwkv_v0[l - 1])] if has_v else [])
            pre = _rwkv_pre(x, start_rows, start_flag, rwkv_mu[l], lora1, lora2, bias, tm=tm_pre)
            xr, xk, xv, lw, ag, gg = pre[:6]
            r = _matmul(xr, rwkv_wr[l], name="rwkv_r")
            k = _matmul(xk, rwkv_wk[l], name="rwkv_k")
            if has_v:
                v = _matmul(xv, rwkv_wv[l], name="rwkv_v", extras=(v_first, pre[6]),
                            epilogue=lambda acc, vf, vg: acc + (vf - acc) * vg)
            else:
                v = _matmul(xv, rwkv_wv[l], name="rwkv_v")
                v_first = v
            seqs = (r, k, v, lw, ag, gg)
            vecs = [row(rwkv_k_k[l]), row(rwkv_k_a[l]), row(rwkv_r_k[l]), row(rwkv_lnx_g[l]), row(rwkv_lnx_b[l])]
            zp, stp = _wkv(seqs, vecs, None, name="wkv_prompt", n_seq=bp, seq_len=tp, row0=0, L=CHUNK, hw=2048)
            zs, sts = _wkv(seqs, vecs, state_wkv[l].astype(F32), name="wkv_sample",
                           n_seq=bs, seq_len=ts, row0=mp, L=ts, hw=1024)
            p_wkv.append(stp.astype(state_wkv.dtype))
            s_wkv.append(sts.astype(state_wkv.dtype))
            z = jnp.concatenate([zp, zs], axis=0)
            x = _matmul(z, bf(rwkv_wo[l]), name="rwkv_out_ln", extras=(x,), vecs=(g0, b0),
                        epilogue=_ln_epilogue(alpha), tm=256, tn=d)
        else:
            i = l - n_a
            q = _matmul(x, attn_wq[i], name="attn_q", out_dtype=BF16)
            table = attn_rel_bias[i]
            op = _attn_prompt(q, k_sh, v_sh, _prompt_bias(table, qb), n_seq=bp, seq_len=tp, qb=qb)
            bias_c, bias_n = _sample_bias(table, kb, ts)
            os_ = _attn_sample(q, k_sh, v_sh, cache_k.reshape(bs, kb, d), cache_v.reshape(bs, kb, d),
                               bias_c, bias_n, n_seq=bs, seq_len=ts, row0=mp)
            o = jnp.concatenate([op, os_], axis=0)
            x = _matmul(o, bf(attn_wo[i]), name="attn_out_ln", extras=(x,), vecs=(g0, b0),
                        epilogue=_ln_epilogue(alpha), tm=256, tn=d)
        j = l // 2
        if l % 2 == 0:
            x = _dense_ffn(x, ffn_w1[j], ffn_w3[j], ffn_w2[j], g1, b1, alpha=alpha, tm=tm_dense)
        else:
            x = _moe(x, moe_router[j], moe_w1[j], moe_w3[j], moe_w2[j], g1, b1, alpha=alpha, tm=tm_moe)
        if l == n_a - 1:
            k_sh = _matmul(x, attn_wk, name="attn_k")
            v_sh = _matmul(x, attn_wv, name="attn_v")

    heads = d // HEAD
    keep = min(PREV, tp)
    kp, vp = _to_heads((k_sh, v_sh), row0=tp - keep, seq_stride=tp, n_seq=bp, rows=keep)
    ks, vs = _to_heads((k_sh, v_sh), row0=mp, seq_stride=ms, n_seq=1, rows=ms)
    return (x[:mp].reshape(bp, tp, d), x[mp:].reshape(bs, ts, d),
            jnp.stack(p_wkv), jnp.stack(p_shift),
            kp.reshape(bp, keep, heads, HEAD), vp.reshape(bp, keep, heads, HEAD),
            jnp.stack(s_wkv), jnp.stack(s_shift),
            ks.reshape(bs, ts, heads, HEAD), vs.reshape(bs, ts, heads, HEAD))
```

```python
import functools
import math

import jax
import jax.numpy as jnp
from jax import lax
from jax.experimental import pallas as pl
from jax.experimental.pallas import tpu as pltpu

F32 = jnp.float32
BF16 = jnp.bfloat16

CHUNK = 64
PREV_CHUNKS = 8
PREV = PREV_CHUNKS * CHUNK
REL_CLIP = 128
HEAD = 64
TOP_K = 2
GN_EPS = 64e-5
LN_EPS = 1e-5
NEG_INF = -1e30

LANES = 128
SUBLANES = 8
VMEM_LIMIT_BYTES = 60 * 1024 * 1024


def _params(*sem):
    return pltpu.CompilerParams(dimension_semantics=sem, vmem_limit_bytes=VMEM_LIMIT_BYTES)


def _tile(n, target, mult=16):
    best = None
    for t in range(mult, min(n, target) + 1, mult):
        if n % t == 0:
            best = t
    assert best is not None, (n, target, mult)
    return best


def _dot(a, b):
    return jnp.dot(a, b, preferred_element_type=F32)


def _dot_nt(a, b):
    return lax.dot_general(a, b, (((1,), (1,)), ((), ())), preferred_element_type=F32)


def _dot_tn(a, b):
    return lax.dot_general(a, b, (((0,), (0,)), ((), ())), preferred_element_type=F32)


def _split_bf16(x):
    hi = x.astype(BF16)
    lo = (x - hi.astype(F32)).astype(BF16)
    return hi, lo


def _layer_norm(y, g, b):
    mu = jnp.mean(y, axis=-1, keepdims=True)
    dev = y - mu
    var = jnp.mean(dev * dev, axis=-1, keepdims=True)
    return dev * lax.rsqrt(var + LN_EPS) * g + b


def _softplus(u):
    return jnp.maximum(u, 0.0) + jnp.log(1.0 + jnp.exp(-jnp.abs(u)))


def _mm_kernel(*refs, n_extra, n_vec, epilogue, cast_w):
    x_ref, w_ref = refs[0], refs[1]
    extra = refs[2:2 + n_extra]
    vecs = refs[2 + n_extra:2 + n_extra + n_vec]
    o_ref = refs[2 + n_extra + n_vec]
    if cast_w:
        wb_ref = refs[-1]

        @pl.when(pl.program_id(1) == 0)
        def _():
            wb_ref[...] = w_ref[...].astype(BF16)

        w = wb_ref[...]
    else:
        w = w_ref[...]
    acc = _dot(x_ref[...].astype(BF16), w)
    if epilogue is not None:
        acc = epilogue(acc, *[e[...] for e in extra], *[v[...] for v in vecs])
    o_ref[...] = acc.astype(o_ref.dtype)


def _matmul(x, w, *, name, layer=None, out_dtype=F32, epilogue=None, extras=(), vecs=(), tm=512, tn=1024):
    m, k = x.shape
    n = w.shape[-1]
    tm = _tile(m, tm)
    tn = n if n <= tn else _tile(n, tn, LANES)
    cast_w = w.dtype != BF16
    if w.ndim == 3:
        w_spec = pl.BlockSpec((None, k, tn), lambda j, i: (layer, 0, j))
    else:
        w_spec = pl.BlockSpec((k, tn), lambda j, i: (0, j))
    in_specs = [pl.BlockSpec((tm, k), lambda j, i: (i, 0)), w_spec]
    in_specs += [pl.BlockSpec((tm, tn), lambda j, i: (i, j)) for _ in extras]
    in_specs += [pl.BlockSpec((1, tn), lambda j, i: (0, j)) for _ in vecs]
    return pl.pallas_call(
        functools.partial(_mm_kernel, n_extra=len(extras), n_vec=len(vecs),
                          epilogue=epilogue, cast_w=cast_w),
        grid=(n // tn, m // tm),
        in_specs=in_specs,
        out_specs=pl.BlockSpec((tm, tn), lambda j, i: (i, j)),
        out_shape=jax.ShapeDtypeStruct((m, n), out_dtype),
        scratch_shapes=[pltpu.VMEM((k, tn), BF16)] if cast_w else [],
        compiler_params=_params("parallel", "arbitrary"),
        name=name,
    )(x, w, *extras, *vecs)


def _ln_epilogue(alpha):
    def epi(acc, xres, g, b):
        return _layer_norm(alpha * xres + acc, g, b)
    return epi


def _rwkv_pre_kernel(*refs, has_v):
    it = iter(refs)
    x_ref, halo_ref, start_ref, flag_ref, mu_ref = (next(it) for _ in range(5))
    w1, a1, g1 = next(it), next(it), next(it)
    v1 = next(it) if has_v else None
    w2, a2, g2 = next(it), next(it), next(it)
    v2 = next(it) if has_v else None
    w0, a0 = next(it), next(it)
    v0 = next(it) if has_v else None
    xr_ref, xk_ref, xv_ref, lw_ref, ag_ref, g_ref = (next(it) for _ in range(6))
    vg_ref = next(it) if has_v else None

    x = x_ref[...]
    row = lax.broadcasted_iota(jnp.int32, (x.shape[0], 1), 0)
    prev = jnp.where(row == 0, halo_ref[SUBLANES - 1:SUBLANES, :], pltpu.roll(x, 1, 0))
    flag = flag_ref[...]
    prev = jnp.where(flag == 1.0, 0.0, jnp.where(flag == 2.0, start_ref[...], prev))
    xx = prev - x
    mix = lambda i: (x + xx * mu_ref[i:i + 1, :]).astype(BF16)
    xr_ref[...] = mix(0)
    xk_ref[...] = mix(2)
    xv = mix(3)
    xv_ref[...] = xv
    hw = jnp.tanh(_dot(mix(1), w1[...]))
    z = w0[...] + _dot(hw.astype(BF16), w2[...])
    lw_ref[...] = -jnp.exp(-_softplus(-z) - 0.5)
    ha = _dot(mix(4), a1[...])
    ag_ref[...] = jax.nn.sigmoid(a0[...] + _dot(ha.astype(BF16), a2[...]))
    hg = jax.nn.sigmoid(_dot(mix(5), g1[...]))
    g_ref[...] = _dot(hg.astype(BF16), g2[...])
    if has_v:
        hv = _dot(xv, v1[...])
        vg_ref[...] = jax.nn.sigmoid(v0[...] + _dot(hv.astype(BF16), v2[...]))


def _rwkv_pre(x, start_rows, flag, mu, lora1, lora2, bias, *, tm):
    m, d = x.shape
    first_start_tile = (m - start_rows.shape[0]) // tm
    has_v = len(lora1) == 4
    row = pl.BlockSpec((tm, d), lambda i: (i, 0))
    halo = pl.BlockSpec((SUBLANES, d), lambda i: (jnp.maximum(i * (tm // SUBLANES) - 1, 0), 0))
    start = pl.BlockSpec((tm, d), lambda i: (jnp.maximum(i - first_start_tile, 0), 0))
    full = lambda a: pl.BlockSpec(a.shape, lambda i: (0,) * a.ndim)
    ins = [x, x, start_rows, flag, mu, *lora1, *lora2, *bias]
    in_specs = [row, halo, start, pl.BlockSpec((tm, 1), lambda i: (i, 0))] + [full(a) for a in ins[4:]]
    out_shape = [jax.ShapeDtypeStruct((m, d), BF16)] * 3 + [jax.ShapeDtypeStruct((m, d), F32)] * (3 + has_v)
    return pl.pallas_call(
        functools.partial(_rwkv_pre_kernel, has_v=has_v),
        grid=(m // tm,),
        in_specs=in_specs,
        out_specs=[row] * len(out_shape),
        out_shape=out_shape,
        compiler_params=_params("parallel"),
        name="rwkv_pre",
    )(*ins)


def _wkv_kernel(*refs, L, npair, nc, has_init):
    it = iter(refs)
    r_ref, k_ref, v_ref, lw_ref, ag_ref, g_ref = (next(it) for _ in range(6))
    kk_ref, ka_ref, rk_ref, lg_ref, lb_ref = (next(it) for _ in range(5))
    s0_ref = next(it) if has_init else None
    z_ref, so_ref, st_ref = next(it), next(it), next(it)
    L2 = 2 * L
    pairs = range(npair)
    bf = lambda t: t.astype(BF16)

    @pl.when(pl.program_id(2) == 0)
    def _():
        if has_init:
            zero = jnp.zeros((HEAD, HEAD), F32)
            for p in pairs:
                top = jnp.concatenate([s0_ref[0, 2 * p], zero], axis=1)
                bot = jnp.concatenate([zero, s0_ref[0, 2 * p + 1]], axis=1)
                st_ref[p] = jnp.concatenate([top, bot], axis=0).T
        else:
            st_ref[...] = jnp.zeros(st_ref.shape, F32)

    def iota(shape, dim):
        return lax.broadcasted_iota(jnp.int32, shape, dim)

    tri = (iota((L, L), 0) >= iota((L, L), 1)).astype(BF16)
    lw = lw_ref[...]
    lw_hi, lw_lo = _split_bf16(lw)
    cum = _dot(tri, lw_hi) + _dot(tri, lw_lo)
    cum_last = cum[L - 1:L, :]
    e_pos = jnp.exp(cum)
    e_neg = jnp.exp(-cum)
    e_prev = jnp.exp(cum - lw)
    e_last = jnp.exp(cum_last - cum)
    lw_split = jnp.concatenate([lw_hi, lw_lo], axis=0)

    rows, cols = iota((L2, L2), 0), iota((L2, L2), 1)
    same_head = (rows >= L) == (cols >= L)
    lower_strict = same_head & (rows > cols)
    lower_incl = same_head & (rows >= cols)
    eye = (rows == cols).astype(F32)
    own_lanes = ((iota((L2, LANES), 0) >= L) == (iota((L2, LANES), 1) >= HEAD)).astype(F32)
    ones = jnp.ones((L2, LANES), BF16)
    dup = lambda t: jnp.concatenate([t, t], axis=0)
    stack = lambda t: dup(t) * own_lanes
    unstack = lambda t: t[:L] + t[L:]
    lanes = lambda p: slice(p * LANES, (p + 1) * LANES)
    left, right = slice(0, LANES), slice(LANES, 2 * LANES)

    ops = []
    for p in pairs:
        sl = lanes(p)
        ag = ag_ref[:, sl]
        kraw = k_ref[:, sl]
        kk = stack(kraw * kk_ref[:, sl])
        kk = kk / jnp.maximum(jnp.sqrt(jnp.sum(kk * kk, axis=-1, keepdims=True)), 1e-12)
        ks = stack(kraw * (1.0 + (ag - 1.0) * ka_ref[:, sl]))
        rs = stack(r_ref[:, sl])
        vs = stack(v_ref[:, sl])
        b = kk * dup(ag)
        en = dup(e_neg[:, sl])
        el = dup(e_last[:, sl])
        ops.append(dict(
            ks=ks, rs=rs, vs=vs, vb=bf(vs),
            at=bf(-kk * dup(e_prev[:, sl])), rt=rs * dup(e_pos[:, sl]),
            bt=bf(b * en), kt=bf(ks * en), bh=bf(b * el), kh=bf(ks * el)))

    for o in ops:
        o["rtb"] = bf(o["rt"])
        if L2 % LANES == 0:
            m = _dot_nt(jnp.concatenate([o["at"], o["rtb"]], axis=0),
                        jnp.concatenate([o["bt"], o["kt"]], axis=0))
            m_ab, m_ak, m_rb, m_rk = m[:L2, :L2], m[:L2, L2:], m[L2:, :L2], m[L2:, L2:]
        else:
            m_ab, m_ak = _dot_nt(o["at"], o["bt"]), _dot_nt(o["at"], o["kt"])
            m_rb, m_rk = _dot_nt(o["rtb"], o["bt"]), _dot_nt(o["rtb"], o["kt"])
        o["m_ab"] = jnp.where(lower_strict, m_ab, 0.0)
        o["m_ak"] = bf(jnp.where(lower_strict, m_ak, 0.0))
        o["m_rb"] = bf(jnp.where(lower_incl, m_rb, 0.0))
        o["m_rk"] = bf(jnp.where(lower_incl, m_rk, 0.0))

    for o in ops:
        o["tinv"] = eye + o["m_ab"]
        o["mp"] = o["m_ab"]
    span = 2
    while span < L:
        for o in ops:
            mpb = bf(o["mp"])
            o["mp"] = _dot(mpb, mpb)
        for o in ops:
            o["tinv"] = o["tinv"] + _dot(bf(o["tinv"]), bf(o["mp"]))
        span *= 2

    for o in ops:
        o["makv"] = bf(_dot(o["m_ak"], o["vb"]))
    for o in ops:
        au = _dot(bf(o["tinv"]), jnp.concatenate([o["at"], o["makv"]], axis=1))
        o["aub"] = bf(au)
    for o in ops:
        ry = _dot(o["m_rb"], o["aub"])
        o["rh"] = bf(o["rt"] + ry[:, left])
        o["y0"] = ry[:, right] + _dot(o["m_rk"], o["vb"])
        gq = _dot_tn(o["bh"], o["aub"])
        o["g"] = bf(gq[:, left])
        o["q"] = gq[:, right] + _dot_tn(o["kh"], o["vb"])
    last = pl.program_id(2) == nc - 1
    for p, o in zip(pairs, ops):
        decay = jnp.exp(_dot_tn(lw_split[:, lanes(p)], ones))
        st = st_ref[p]
        stb = bf(st)
        o["ys"] = _dot(o["rh"], stb) + o["y0"]
        st_ref[p] = decay * st + _dot(o["g"], stb) + o["q"]

    @pl.when(last)
    def _():
        for p in pairs:
            t = st_ref[p].T
            so_ref[0, 2 * p] = t[:HEAD, :HEAD]
            so_ref[0, 2 * p + 1] = t[HEAD:, HEAD:]

    for p, o in zip(pairs, ops):
        sl = lanes(p)
        ys = o["ys"]
        mean = jnp.sum(ys, axis=-1, keepdims=True) * (1.0 / HEAD)
        dev = (ys - mean) * own_lanes
        var = jnp.sum(dev * dev, axis=-1, keepdims=True) * (1.0 / HEAD)
        yn = unstack(dev * lax.rsqrt(var + GN_EPS)) * lg_ref[:, sl] + lb_ref[:, sl]
        bonus = unstack(jnp.sum(o["rs"] * o["ks"] * rk_ref[:, sl], axis=-1, keepdims=True) * o["vs"])
        z_ref[:, sl] = ((yn + bonus) * g_ref[:, sl]).astype(z_ref.dtype)


def _wkv(seqs, vecs, s0, *, name, n_seq, seq_len, row0, L, hw):
    d = seqs[0].shape[1]
    hw = min(hw, d)
    nc = seq_len // L
    blk0 = row0 // L
    npair = hw // LANES
    seq_spec = pl.BlockSpec((L, hw), lambda b, h, c: (blk0 + b * nc + c, h))
    vec_spec = pl.BlockSpec((1, hw), lambda b, h, c: (0, h))
    st_spec = pl.BlockSpec((1, 2 * npair, HEAD, HEAD), lambda b, h, c: (b, h, 0, 0))
    in_specs = [seq_spec] * 6 + [vec_spec] * 5 + ([st_spec] if s0 is not None else [])
    ins = list(seqs) + list(vecs) + ([s0] if s0 is not None else [])
    return pl.pallas_call(
        functools.partial(_wkv_kernel, L=L, npair=npair, nc=nc, has_init=s0 is not None),
        grid=(n_seq, d // hw, nc),
        in_specs=in_specs,
        out_specs=[pl.BlockSpec((L, hw), lambda b, h, c: (b * nc + c, h)), st_spec],
        out_shape=[jax.ShapeDtypeStruct((n_seq * seq_len, d), BF16),
                   jax.ShapeDtypeStruct((n_seq, d // HEAD, HEAD, HEAD), F32)],
        scratch_shapes=[pltpu.VMEM((npair, LANES, LANES), F32)],
        compiler_params=_params("parallel", "parallel", "arbitrary"),
        name=name,
    )(*ins)


def _head_masks():
    lane = lax.broadcasted_iota(jnp.int32, (1, LANES), 1)
    return lane < HEAD


def _attn_prompt_kernel(q_ref, k0, k1, k2, v0, v1, v2, bias_ref, o_ref, *, qb):
    first = _head_masks()
    blk = pl.program_id(2)
    q = q_ref[...]
    kcat = jnp.concatenate([k0[...], k1[...], k2[...]], axis=0).astype(BF16)
    vcat = jnp.concatenate([v0[...], v1[...], v2[...]], axis=0).astype(BF16)
    kpos = lax.broadcasted_iota(jnp.int32, (1, 3 * qb), 1) + (blk - 2) * qb
    valid = kpos >= 0
    outs = []
    for h in range(2):
        qh = jnp.where(first if h == 0 else ~first, q, 0.0).astype(BF16)
        s = _dot_nt(qh, kcat) * (HEAD ** -0.5) + bias_ref[h]
        s = jnp.where(valid, s, NEG_INF)
        p = jnp.exp(s - jnp.max(s, axis=-1, keepdims=True))
        denom = jnp.sum(p, axis=-1, keepdims=True)
        outs.append(_dot(p.astype(BF16), vcat) / denom)
    o_ref[...] = jnp.where(first, outs[0], outs[1]).astype(o_ref.dtype)


def _attn_prompt(q, k, v, bias, *, n_seq, seq_len, qb):
    d = q.shape[1]
    nq = seq_len // qb
    q_spec = pl.BlockSpec((qb, LANES), lambda h, b, i: (b * nq + i, h))
    kv = lambda off: pl.BlockSpec((qb, LANES), lambda h, b, i: (b * nq + jnp.maximum(i - off, 0), h))
    return pl.pallas_call(
        functools.partial(_attn_prompt_kernel, qb=qb),
        grid=(d // LANES, n_seq, nq),
        in_specs=[q_spec, kv(2), kv(1), kv(0), kv(2), kv(1), kv(0),
                  pl.BlockSpec((2, qb, 3 * qb), lambda h, b, i: (h, 0, 0))],
        out_specs=pl.BlockSpec((qb, LANES), lambda h, b, i: (b * nq + i, h)),
        out_shape=jax.ShapeDtypeStruct((n_seq * seq_len, d), BF16),
        compiler_params=_params("parallel", "parallel", "arbitrary"),
        name="attn_prompt",
    )(q, k, k, k, v, v, v, bias)


def _attn_sample_kernel(q_ref, kn_ref, vn_ref, kc_ref, vc_ref, bc_ref, bn_ref, o_ref, *, npair):
    first = _head_masks()
    lanes = lambda p: slice(p * LANES, (p + 1) * LANES)
    units = []
    for p in range(npair):
        q = q_ref[:, lanes(p)]
        kc, kn = kc_ref[0, :, lanes(p)].astype(BF16), kn_ref[:, lanes(p)].astype(BF16)
        for h in range(2):
            qh = jnp.where(first if h == 0 else ~first, q, 0.0).astype(BF16)
            units.append(dict(p=p, h=2 * p + h, sc=_dot_nt(qh, kc), sn=_dot_nt(qh, kn)))
    for u in units:
        sc = u["sc"] * (HEAD ** -0.5) + bc_ref[u["h"]]
        sn = u["sn"] * (HEAD ** -0.5) + bn_ref[u["h"]]
        mx = jnp.maximum(jnp.max(sc, axis=-1, keepdims=True), jnp.max(sn, axis=-1, keepdims=True))
        pc, pn = jnp.exp(sc - mx), jnp.exp(sn - mx)
        u["denom"] = jnp.sum(pc, axis=-1, keepdims=True) + jnp.sum(pn, axis=-1, keepdims=True)
        u["pc"], u["pn"] = pc.astype(BF16), pn.astype(BF16)
    for u in units:
        vc, vn = vc_ref[0, :, lanes(u["p"])].astype(BF16), vn_ref[:, lanes(u["p"])].astype(BF16)
        u["o"] = (_dot(u["pc"], vc) + _dot(u["pn"], vn)) / u["denom"]
    for p in range(npair):
        o_ref[:, lanes(p)] = jnp.where(first, units[2 * p]["o"], units[2 * p + 1]["o"]).astype(o_ref.dtype)


def _attn_sample(q, k, v, cache_k, cache_v, bias_c, bias_n, *, n_seq, seq_len, row0):
    d = q.shape[1]
    kb = cache_k.shape[1]
    blk0 = row0 // seq_len
    new = pl.BlockSpec((seq_len, d), lambda s: (blk0 + s, 0))
    cache = pl.BlockSpec((1, kb, d), lambda s: (s, 0, 0))
    full = lambda a: pl.BlockSpec(a.shape, lambda s: (0,) * a.ndim)
    return pl.pallas_call(
        functools.partial(_attn_sample_kernel, npair=d // LANES),
        grid=(n_seq,),
        in_specs=[new, new, new, cache, cache, full(bias_c), full(bias_n)],
        out_specs=pl.BlockSpec((seq_len, d), lambda s: (s, 0)),
        out_shape=jax.ShapeDtypeStruct((n_seq * seq_len, d), BF16),
        compiler_params=_params("parallel"),
        name="attn_sample",
    )(q, k, v, cache_k, cache_v, bias_c, bias_n)


def _to_heads_kernel(*refs):
    n = len(refs) // 2
    for x_ref, o_ref in zip(refs[:n], refs[n:]):
        o_ref[...] = x_ref[...].reshape(o_ref.shape)


def _to_heads(arrs, *, row0, seq_stride, n_seq, rows):
    d = arrs[0].shape[1]
    tb = _tile(math.gcd(math.gcd(row0, seq_stride), rows), LANES, SUBLANES)
    in_spec = pl.BlockSpec((tb, d), lambda s, i: ((row0 + s * seq_stride) // tb + i, 0))
    out_spec = pl.BlockSpec((tb, d // HEAD, HEAD), lambda s, i: (s * (rows // tb) + i, 0, 0))
    return pl.pallas_call(
        _to_heads_kernel,
        grid=(n_seq, rows // tb),
        in_specs=[in_spec] * len(arrs),
        out_specs=[out_spec] * len(arrs),
        out_shape=[jax.ShapeDtypeStruct((n_seq * rows, d // HEAD, HEAD), a.dtype) for a in arrs],
        compiler_params=_params("parallel", "parallel"),
        name="to_heads",
    )(*arrs)


def _prompt_bias(table, qb):
    width = 3 * qb
    period = width + qb
    dist = (2 * qb - jnp.arange(period)) % period
    dist = jnp.where(dist > width, dist - period, dist)
    by_dist = table[:, jnp.clip(dist, -REL_CLIP, REL_CLIP) + REL_CLIP].astype(F32)
    heads = table.shape[0]
    rolled = jnp.tile(by_dist, (1, qb))[:, :qb * (period - 1)].reshape(heads, qb, period - 1)
    bias = rolled[:, :, :width]
    qi = jnp.arange(qb)[:, None]
    kj = jnp.arange(width)[None, :] - 2 * qb
    qc, kc = qi // CHUNK, jnp.floor_divide(kj, CHUNK)
    band = (kc <= qc) & (kc >= qc - PREV_CHUNKS)
    return jnp.where(band[None], bias, NEG_INF)


def _sample_bias(table, kb, t):
    dist = (kb + jnp.arange(t))[:, None] - jnp.arange(kb + t)[None, :]
    bias = table[:, jnp.clip(dist, -REL_CLIP, REL_CLIP) + REL_CLIP].astype(F32)
    return bias[:, :, :kb], bias[:, :, kb:]


def _swiglu_step(xb, w1, w3, w2):
    h1 = _dot(xb, w1)
    h3 = _dot(xb, w3)
    return _dot((h1 * jax.nn.sigmoid(h1) * h3).astype(BF16), w2)


def _dense_ffn_kernel(x_ref, w1_ref, w3_ref, w2_ref, g_ref, b_ref, o_ref, xb_ref, *, nf, alpha):
    f = pl.program_id(1)

    @pl.when(f == 0)
    def _():
        o_ref[...] = jnp.zeros(o_ref.shape, F32)
        xb_ref[...] = x_ref[...].astype(BF16)

    o_ref[...] += _swiglu_step(xb_ref[...], w1_ref[...].astype(BF16), w3_ref[...].astype(BF16),
                               w2_ref[...].astype(BF16))

    @pl.when(f == nf - 1)
    def _():
        o_ref[...] = _layer_norm(alpha * x_ref[...] + o_ref[...], g_ref[...], b_ref[...])


def _dense_ffn(x, w1, w3, w2, g, b, *, alpha, tm, tf=512):
    rows, d = x.shape
    dff = w1.shape[1]
    tf = _tile(dff, tf, LANES)
    nf = dff // tf
    row = pl.BlockSpec((tm, d), lambda t, f: (t, 0))
    vec = pl.BlockSpec((1, d), lambda t, f: (0, 0))
    w13 = pl.BlockSpec((d, tf), lambda t, f: (0, f))
    return pl.pallas_call(
        functools.partial(_dense_ffn_kernel, nf=nf, alpha=alpha),
        grid=(rows // tm, nf),
        in_specs=[row, w13, w13, pl.BlockSpec((tf, d), lambda t, f: (f, 0)), vec, vec],
        out_specs=row,
        out_shape=jax.ShapeDtypeStruct((rows, d), F32),
        scratch_shapes=[pltpu.VMEM((tm, d), BF16)],
        compiler_params=_params("parallel", "arbitrary"),
        name="dense_ffn",
    )(x, w1.astype(BF16), w3.astype(BF16), w2.astype(BF16), g, b)


def _router_kernel(x_ref, w_ref, o_ref, *, n_exp):
    x_hi, x_lo = _split_bf16(x_ref[...])
    w_hi, w_lo = _split_bf16(w_ref[...])
    logits = _dot(x_hi, w_hi) + _dot(x_hi, w_lo) + _dot(x_lo, w_hi)
    lane = lax.broadcasted_iota(jnp.int32, logits.shape, 1).astype(F32)
    valid = lane < n_exp
    lg = jnp.where(valid, logits, NEG_INF)
    ex = jnp.where(valid, jnp.exp(lg - jnp.max(lg, axis=-1, keepdims=True)), 0.0)
    probs = jnp.where(valid, ex / jnp.sum(ex, axis=-1, keepdims=True), -1.0)
    p1 = jnp.max(probs, axis=-1, keepdims=True)
    i1 = jnp.min(jnp.where(probs == p1, lane, float(LANES)), axis=-1, keepdims=True)
    rest = jnp.where(lane == i1, -1.0, probs)
    p2 = jnp.max(rest, axis=-1, keepdims=True)
    i2 = jnp.min(jnp.where(rest == p2, lane, float(LANES)), axis=-1, keepdims=True)
    tot = p1 + p2
    out = jnp.where(lane == 0, i1,
                    jnp.where(lane == 1, i2,
                              jnp.where(lane == 2, p1 / tot, jnp.where(lane == 3, p2 / tot, 0.0))))
    o_ref[...] = out


def _router(x, w, *, tm=512):
    m, d = x.shape
    n_exp = w.shape[1]
    wp = jnp.zeros((d, LANES), F32).at[:, :n_exp].set(w)
    tm = _tile(m, tm)
    return pl.pallas_call(
        functools.partial(_router_kernel, n_exp=n_exp),
        grid=(m // tm,),
        in_specs=[pl.BlockSpec((tm, d), lambda i: (i, 0)), pl.BlockSpec((d, LANES), lambda i: (0, 0))],
        out_specs=pl.BlockSpec((tm, LANES), lambda i: (i, 0)),
        out_shape=jax.ShapeDtypeStruct((m, LANES), F32),
        compiler_params=_params("parallel"),
        name="router",
    )(x, wp)


DMA_UNROLL = 8


def _row_copy(src_ref, dst_ref, sem, src_row, dst_row):
    return pltpu.make_async_copy(src_ref.at[pl.ds(src_row, 1)], dst_ref.at[pl.ds(dst_row, 1)], sem)


def _moe_ffn_kernel(te_ref, rows_ref, na_ref, src_ref, x_hbm, w1_ref, w3_ref, w2_ref, o_ref,
                    gbuf, xb_ref, sem, *, n_tiles, tm, sub, n_issue):
    t, f = pl.program_id(0), pl.program_id(1)
    rows = rows_ref[t]
    chunk = tm // n_issue

    def issue(tile, lo, n):
        def body(i, c):
            _row_copy(x_hbm, gbuf, sem, src_ref[tile * tm + lo + i], lo + i).start()
            return c
        lax.fori_loop(0, n, body, 0, unroll=DMA_UNROLL)

    def wait_tile():
        def body(i, c):
            _row_copy(x_hbm, gbuf, sem, 0, i).wait()
            return c
        lax.fori_loop(0, tm, body, 0, unroll=DMA_UNROLL)

    @pl.when(jnp.logical_and(jnp.logical_and(t == 0, f == 0), rows > 0))
    def _():
        issue(0, 0, tm)

    @pl.when(f == 0)
    def _():
        o_ref[...] = jnp.zeros(o_ref.shape, F32)

        @pl.when(rows > 0)
        def _():
            wait_tile()
            xb_ref[...] = gbuf[...].astype(BF16)

    nxt = jnp.minimum(t + 1, n_tiles - 1)
    fetch_next = jnp.logical_and(jnp.logical_and(f >= 1, f <= n_issue),
                                 jnp.logical_and(t + 1 < n_tiles, rows_ref[nxt] > 0))

    @pl.when(fetch_next)
    def _():
        issue(t + 1, (f - 1) * chunk, chunk)

    for n in range(sub, tm + 1, sub):
        @pl.when(jnp.logical_and(rows > n - sub, rows <= n))
        def _():
            o_ref[:n, :] += _swiglu_step(xb_ref[:n, :], w1_ref[...].astype(BF16),
                                         w3_ref[...].astype(BF16), w2_ref[...].astype(BF16))


def _moe_ffn(x, src, w1, w3, w2, layer, tile_expert, tile_rows, n_active, *, tm, tf):
    d = x.shape[1]
    n_slots = src.shape[0]
    n_tiles = n_slots // tm
    dff = w1.shape[3]
    tf = _tile(dff, min(tf, dff // 2), LANES)
    nf = dff // tf
    n_issue = 1
    while 2 * n_issue <= max(nf - 1, 1) and tm % (2 * n_issue) == 0:
        n_issue *= 2
    assert nf >= 2
    sub = min(tm, 256)

    def clamp(t, na):
        return jnp.minimum(t, na[0] - 1)

    def f_idx(t, f, na):
        return jnp.where(t < na[0], f, nf - 1)

    w13 = pl.BlockSpec((None, None, d, tf),
                       lambda t, f, te, tr, na, sr: (layer, te[clamp(t, na)], 0, f_idx(t, f, na)))
    w2s = pl.BlockSpec((None, None, tf, d),
                       lambda t, f, te, tr, na, sr: (layer, te[clamp(t, na)], f_idx(t, f, na), 0))
    return pl.pallas_call(
        functools.partial(_moe_ffn_kernel, n_tiles=n_tiles, tm=tm, sub=sub, n_issue=n_issue),
        grid_spec=pltpu.PrefetchScalarGridSpec(
            num_scalar_prefetch=4,
            grid=(n_tiles, nf),
            in_specs=[pl.BlockSpec(memory_space=pl.ANY), w13, w13, w2s],
            out_specs=pl.BlockSpec((tm, d), lambda t, f, te, tr, na, sr: (t, 0)),
            scratch_shapes=[pltpu.VMEM((tm, d), F32), pltpu.VMEM((tm, d), BF16),
                            pltpu.SemaphoreType.DMA(())]),
        out_shape=jax.ShapeDtypeStruct((n_slots, d), F32),
        compiler_params=_params("arbitrary", "arbitrary"),
        name="moe_ffn",
    )(tile_expert, tile_rows, n_active, src, x, w1, w3, w2)


def _combine_kernel(i0_ref, i1_ref, y_ref, x_ref, sel_ref, g_ref, b_ref, o_ref, buf, sem, *, tc, alpha):
    base = pl.program_id(0) * tc

    def start(i, c):
        _row_copy(y_ref, buf.at[0], sem, i0_ref[base + i], i).start()
        _row_copy(y_ref, buf.at[1], sem, i1_ref[base + i], i).start()
        return c

    def wait(i, c):
        _row_copy(y_ref, buf.at[0], sem, 0, i).wait()
        _row_copy(y_ref, buf.at[1], sem, 0, i).wait()
        return c

    lax.fori_loop(0, tc, start, 0, unroll=DMA_UNROLL)
    lax.fori_loop(0, tc, wait, 0, unroll=DMA_UNROLL)
    sel = sel_ref[...]
    mixed = buf[0] * sel[:, TOP_K:TOP_K + 1] + buf[1] * sel[:, TOP_K + 1:TOP_K + 2]
    o_ref[...] = _layer_norm(alpha * x_ref[...] + mixed, g_ref[...], b_ref[...])


def _combine(y, x, sel, i0, i1, g, b, *, alpha, tc=256):
    m, d = x.shape
    tc = _tile(m, tc)
    row = pl.BlockSpec((tc, d), lambda i, a, c: (i, 0))
    vec = pl.BlockSpec((1, d), lambda i, a, c: (0, 0))
    return pl.pallas_call(
        functools.partial(_combine_kernel, tc=tc, alpha=alpha),
        grid_spec=pltpu.PrefetchScalarGridSpec(
            num_scalar_prefetch=2,
            grid=(m // tc,),
            in_specs=[pl.BlockSpec(memory_space=pl.ANY), row,
                      pl.BlockSpec((tc, LANES), lambda i, a, c: (i, 0)), vec, vec],
            out_specs=row,
            scratch_shapes=[pltpu.VMEM((2, tc, d), F32), pltpu.SemaphoreType.DMA(())]),
        out_shape=jax.ShapeDtypeStruct((m, d), F32),
        compiler_params=_params("arbitrary"),
        name="moe_combine",
    )(i0, i1, y, x, sel, g, b)


def _moe(x, router, w1, w3, w2, layer, g, b, *, alpha, tm):
    m, d = x.shape
    n_exp = w1.shape[1]
    sel = _router(x, router)
    top_i = sel[:, :TOP_K].astype(jnp.int32)
    e_flat = top_i.reshape(-1)
    onehot = (e_flat[:, None] == jnp.arange(n_exp)[None, :]).astype(jnp.int32)
    rank = jnp.sum((jnp.cumsum(onehot, axis=0) - onehot) * onehot, axis=1)
    counts = jnp.sum(onehot, axis=0)
    padded = (counts + tm - 1) // tm * tm
    ends = jnp.cumsum(padded)
    slot = jnp.sum(onehot * (ends - padded)[None, :], axis=1) + rank
    n_slots = (TOP_K * m + n_exp * (tm - 1)) // tm * tm
    n_tiles = n_slots // tm
    src = jnp.zeros((n_slots,), jnp.int32).at[slot].set(jnp.arange(TOP_K * m, dtype=jnp.int32) // TOP_K)
    tile_start = jnp.arange(n_tiles, dtype=jnp.int32) * tm
    tile_expert = jnp.minimum(jnp.sum((ends[None, :] <= tile_start[:, None]).astype(jnp.int32), axis=1),
                              n_exp - 1).astype(jnp.int32)
    group_fill = (ends - padded + counts).astype(jnp.int32)
    tile_rows = jnp.clip(group_fill[tile_expert] - tile_start, 0, tm).astype(jnp.int32)
    n_active = (ends[-1:] // tm).astype(jnp.int32)

    ys = _moe_ffn(x, src, w1, w3, w2, layer, tile_expert, tile_rows, n_active, tm=tm, tf=512)
    slot2 = slot.reshape(m, TOP_K).astype(jnp.int32)
    return _combine(ys, x, sel, slot2[:, 0], slot2[:, 1], g, b, alpha=alpha)


def kernel(x_prompt, x_sample, state_wkv, state_shift, cache_k, cache_v, ln_g, ln_b, rwkv_mu, rwkv_wr, rwkv_wk, rwkv_wv, rwkv_wo, rwkv_w0, rwkv_w1, rwkv_w2, rwkv_a0, rwkv_a1, rwkv_a2, rwkv_v0, rwkv_v1, rwkv_v2, rwkv_g1, rwkv_g2, rwkv_k_k, rwkv_k_a, rwkv_r_k, rwkv_lnx_g, rwkv_lnx_b, attn_wk, attn_wv, attn_wq, attn_wo, attn_rel_bias, ffn_w1, ffn_w3, ffn_w2, moe_router, moe_w1, moe_w3, moe_w2):
    bp, tp, d = x_prompt.shape
    bs, ts, _ = x_sample.shape
    depth = ln_g.shape[0]
    n_a = rwkv_wr.shape[0]
    alpha = (2 * depth) ** 0.25
    mp, ms = bp * tp, bs * ts
    m = mp + ms
    kb = cache_k.shape[1]
    assert tp % CHUNK == 0 and d % LANES == 0 and mp % ts == 0
    row = lambda a: a.reshape(1, -1)
    bf = lambda a: a.astype(BF16)
    qb = _tile(tp, 256, CHUNK)
    assert 2 * qb >= PREV
    tm_dense = _tile(m, 640)
    tm_moe = 1024 if m >= 4096 else 64
    tm_pre = _tile(math.gcd(mp, ms), 256)

    x = jnp.concatenate([x_prompt.reshape(mp, d), x_sample.reshape(ms, d)], axis=0)
    ridx = jnp.arange(m)
    start_flag = jnp.where(ridx < mp, (ridx % tp == 0) * 1.0, ((ridx - mp) % ts == 0) * 2.0)
    start_flag = start_flag.astype(F32).reshape(m, 1)

    p_wkv, s_wkv, p_shift, s_shift = [], [], [], []
    v_first = k_sh = v_sh = None
    for l in range(depth):
        g0, b0, g1, b1 = row(ln_g[l, 0]), row(ln_b[l, 0]), row(ln_g[l, 1]), row(ln_b[l, 1])
        if l < n_a:
            p_shift.append(x[:mp].reshape(bp, tp, d)[:, -1])
            s_shift.append(x[mp:].reshape(bs, ts, d)[:, -1])
            start_rows = jnp.repeat(state_shift[l], ts, axis=0)
            has_v = l > 0
            lora1 = [bf(rwkv_w1[l]), bf(rwkv_a1[l]), bf(rwkv_g1[l])] + ([bf(rwkv_v1[l - 1])] if has_v else [])
            lora2 = [bf(rwkv_w2[l]), bf(rwkv_a2[l]), bf(rwkv_g2[l])] + ([bf(rwkv_v2[l - 1])] if has_v else [])
            bias = [row(rwkv_w0[l]), row(rwkv_a0[l])] + ([row(rwkv_v0[l - 1])] if has_v else [])
            pre = _rwkv_pre(x, start_rows, start_flag, rwkv_mu[l], lora1, lora2, bias, tm=tm_pre)
            xr, xk, xv, lw, ag, gg = pre[:6]
            r = _matmul(xr, rwkv_wr, layer=l, name="rwkv_r")
            k = _matmul(xk, rwkv_wk, layer=l, name="rwkv_k")
            if has_v:
                v = _matmul(xv, rwkv_wv, layer=l, name="rwkv_v", extras=(v_first, pre[6]),
                            epilogue=lambda acc, vf, vg: acc + (vf - acc) * vg)
            else:
                v = _matmul(xv, rwkv_wv, layer=l, name="rwkv_v")
                v_first = v
            seqs = (r, k, v, lw, ag, gg)
            vecs = [row(rwkv_k_k[l]), row(rwkv_k_a[l]), row(rwkv_r_k[l]), row(rwkv_lnx_g[l]), row(rwkv_lnx_b[l])]
            zp, stp = _wkv(seqs, vecs, None, name="wkv_prompt", n_seq=bp, seq_len=tp, row0=0, L=CHUNK, hw=2048)
            zs, sts = _wkv(seqs, vecs, state_wkv[l].astype(F32), name="wkv_sample",
                           n_seq=bs, seq_len=ts, row0=mp, L=ts, hw=1024)
            p_wkv.append(stp.astype(state_wkv.dtype))
            s_wkv.append(sts.astype(state_wkv.dtype))
            z = jnp.concatenate([zp, zs], axis=0)
            x = _matmul(z, bf(rwkv_wo[l]), name="rwkv_out_ln", extras=(x,), vecs=(g0, b0),
                        epilogue=_ln_epilogue(alpha), tm=256, tn=d)
        else:
            i = l - n_a
            q = _matmul(x, attn_wq, layer=i, name="attn_q", out_dtype=BF16)
            table = attn_rel_bias[i]
            op = _attn_prompt(q, k_sh, v_sh, _prompt_bias(table, qb), n_seq=bp, seq_len=tp, qb=qb)
            bias_c, bias_n = _sample_bias(table, kb, ts)
            os_ = _attn_sample(q, k_sh, v_sh, cache_k.reshape(bs, kb, d), cache_v.reshape(bs, kb, d),
                               bias_c, bias_n, n_seq=bs, seq_len=ts, row0=mp)
            o = jnp.concatenate([op, os_], axis=0)
            x = _matmul(o, bf(attn_wo[i]), name="attn_out_ln", extras=(x,), vecs=(g0, b0),
                        epilogue=_ln_epilogue(alpha), tm=256, tn=d)
        j = l // 2
        if l % 2 == 0:
            x = _dense_ffn(x, ffn_w1[j], ffn_w3[j], ffn_w2[j], g1, b1, alpha=alpha, tm=tm_dense)
        else:
            x = _moe(x, moe_router[j], moe_w1, moe_w3, moe_w2, j, g1, b1, alpha=alpha, tm=tm_moe)
        if l == n_a - 1:
            k_sh = _matmul(x, attn_wk, name="attn_k")
            v_sh = _matmul(x, attn_wv, name="attn_v")

    heads = d // HEAD
    keep = min(PREV, tp)
    kp, vp = _to_heads((k_sh, v_sh), row0=tp - keep, seq_stride=tp, n_seq=bp, rows=keep)
    ks, vs = _to_heads((k_sh, v_sh), row0=mp, seq_stride=ms, n_seq=1, rows=ms)
    return (x[:mp].reshape(bp, tp, d), x[mp:].reshape(bs, ts, d),
            jnp.stack(p_wkv), jnp.stack(p_shift),
            kp.reshape(bp, keep, heads, HEAD), vp.reshape(bp, keep, heads, HEAD),
            jnp.stack(s_wkv), jnp.stack(s_shift),
            ks.reshape(bs, ts, heads, HEAD), vs.reshape(bs, ts, heads, HEAD))
```

```python
import functools
import math

import jax
import jax.numpy as jnp
from jax import lax
from jax.experimental import pallas as pl
from jax.experimental.pallas import tpu as pltpu

F32 = jnp.float32
BF16 = jnp.bfloat16

CHUNK = 64
PREV_CHUNKS = 8
PREV = PREV_CHUNKS * CHUNK
REL_CLIP = 128
HEAD = 64
TOP_K = 2
GN_EPS = 64e-5
LN_EPS = 1e-5
NEG_INF = -1e30

LANES = 128
SUBLANES = 8
VMEM_LIMIT_BYTES = 60 * 1024 * 1024


def _params(*sem):
    return pltpu.CompilerParams(dimension_semantics=sem, vmem_limit_bytes=VMEM_LIMIT_BYTES)


def _tile(n, target, mult=16):
    best = None
    for t in range(mult, min(n, target) + 1, mult):
        if n % t == 0:
            best = t
    assert best is not None, (n, target, mult)
    return best


def _dot(a, b):
    return jnp.dot(a, b, preferred_element_type=F32)


def _dot_nt(a, b):
    return lax.dot_general(a, b, (((1,), (1,)), ((), ())), preferred_element_type=F32)


def _dot_tn(a, b):
    return lax.dot_general(a, b, (((0,), (0,)), ((), ())), preferred_element_type=F32)


def _split_bf16(x):
    hi = x.astype(BF16)
    lo = (x - hi.astype(F32)).astype(BF16)
    return hi, lo


def _layer_norm(y, g, b):
    mu = jnp.mean(y, axis=-1, keepdims=True)
    dev = y - mu
    var = jnp.mean(dev * dev, axis=-1, keepdims=True)
    return dev * lax.rsqrt(var + LN_EPS) * g + b


def _softplus(u):
    return jnp.maximum(u, 0.0) + jnp.log(1.0 + jnp.exp(-jnp.abs(u)))


def _mm_kernel(*refs, n_extra, n_vec, epilogue, cast_w):
    x_ref, w_ref = refs[0], refs[1]
    extra = refs[2:2 + n_extra]
    vecs = refs[2 + n_extra:2 + n_extra + n_vec]
    o_ref = refs[2 + n_extra + n_vec]
    if cast_w:
        wb_ref = refs[-1]

        @pl.when(pl.program_id(1) == 0)
        def _():
            wb_ref[...] = w_ref[...].astype(BF16)

        w = wb_ref[...]
    else:
        w = w_ref[...]
    acc = _dot(x_ref[...].astype(BF16), w)
    if epilogue is not None:
        acc = epilogue(acc, *[e[...] for e in extra], *[v[...] for v in vecs])
    o_ref[...] = acc.astype(o_ref.dtype)


def _matmul(x, w, *, name, layer=None, out_dtype=F32, epilogue=None, extras=(), vecs=(), tm=512, tn=1024):
    m, k = x.shape
    n = w.shape[-1]
    tm = _tile(m, tm)
    tn = n if n <= tn else _tile(n, tn, LANES)
    cast_w = w.dtype != BF16
    if w.ndim == 3:
        w_spec = pl.BlockSpec((None, k, tn), lambda j, i: (layer, 0, j))
    else:
        w_spec = pl.BlockSpec((k, tn), lambda j, i: (0, j))
    in_specs = [pl.BlockSpec((tm, k), lambda j, i: (i, 0)), w_spec]
    in_specs += [pl.BlockSpec((tm, tn), lambda j, i: (i, j)) for _ in extras]
    in_specs += [pl.BlockSpec((1, tn), lambda j, i: (0, j)) for _ in vecs]
    return pl.pallas_call(
        functools.partial(_mm_kernel, n_extra=len(extras), n_vec=len(vecs),
                          epilogue=epilogue, cast_w=cast_w),
        grid=(n // tn, m // tm),
        in_specs=in_specs,
        out_specs=pl.BlockSpec((tm, tn), lambda j, i: (i, j)),
        out_shape=jax.ShapeDtypeStruct((m, n), out_dtype),
        scratch_shapes=[pltpu.VMEM((k, tn), BF16)] if cast_w else [],
        compiler_params=_params("parallel", "arbitrary"),
        name=name,
    )(x, w, *extras, *vecs)


def _ln_epilogue(alpha):
    def epi(acc, xres, g, b):
        return _layer_norm(alpha * xres + acc, g, b)
    return epi


def _rwkv_pre_kernel(*refs, has_v):
    it = iter(refs)
    x_ref, halo_ref, start_ref, flag_ref, mu_ref = (next(it) for _ in range(5))
    w1, a1, g1 = next(it), next(it), next(it)
    v1 = next(it) if has_v else None
    w2, a2, g2 = next(it), next(it), next(it)
    v2 = next(it) if has_v else None
    w0, a0 = next(it), next(it)
    v0 = next(it) if has_v else None
    xr_ref, xk_ref, xv_ref, lw_ref, ag_ref, g_ref = (next(it) for _ in range(6))
    vg_ref = next(it) if has_v else None

    x = x_ref[...]
    row = lax.broadcasted_iota(jnp.int32, (x.shape[0], 1), 0)
    prev = jnp.where(row == 0, halo_ref[SUBLANES - 1:SUBLANES, :], pltpu.roll(x, 1, 0))
    flag = flag_ref[...]
    prev = jnp.where(flag == 1.0, 0.0, jnp.where(flag == 2.0, start_ref[...], prev))
    xx = prev - x
    mix = lambda i: (x + xx * mu_ref[i:i + 1, :]).astype(BF16)
    xr_ref[...] = mix(0)
    xk_ref[...] = mix(2)
    xv = mix(3)
    xv_ref[...] = xv
    hw = jnp.tanh(_dot(mix(1), w1[...]))
    z = w0[...] + _dot(hw.astype(BF16), w2[...])
    lw_ref[...] = -jnp.exp(-_softplus(-z) - 0.5)
    ha = _dot(mix(4), a1[...])
    ag_ref[...] = jax.nn.sigmoid(a0[...] + _dot(ha.astype(BF16), a2[...]))
    hg = jax.nn.sigmoid(_dot(mix(5), g1[...]))
    g_ref[...] = _dot(hg.astype(BF16), g2[...])
    if has_v:
        hv = _dot(xv, v1[...])
        vg_ref[...] = jax.nn.sigmoid(v0[...] + _dot(hv.astype(BF16), v2[...]))


def _rwkv_pre(x, start_rows, flag, mu, lora1, lora2, bias, *, tm):
    m, d = x.shape
    first_start_tile = (m - start_rows.shape[0]) // tm
    has_v = len(lora1) == 4
    row = pl.BlockSpec((tm, d), lambda i: (i, 0))
    halo = pl.BlockSpec((SUBLANES, d), lambda i: (jnp.maximum(i * (tm // SUBLANES) - 1, 0), 0))
    start = pl.BlockSpec((tm, d), lambda i: (jnp.maximum(i - first_start_tile, 0), 0))
    full = lambda a: pl.BlockSpec(a.shape, lambda i: (0,) * a.ndim)
    ins = [x, x, start_rows, flag, mu, *lora1, *lora2, *bias]
    in_specs = [row, halo, start, pl.BlockSpec((tm, 1), lambda i: (i, 0))] + [full(a) for a in ins[4:]]
    out_shape = [jax.ShapeDtypeStruct((m, d), BF16)] * 3 + [jax.ShapeDtypeStruct((m, d), F32)] * (3 + has_v)
    return pl.pallas_call(
        functools.partial(_rwkv_pre_kernel, has_v=has_v),
        grid=(m // tm,),
        in_specs=in_specs,
        out_specs=[row] * len(out_shape),
        out_shape=out_shape,
        compiler_params=_params("parallel"),
        name="rwkv_pre",
    )(*ins)


def _wkv_kernel(*refs, L, npair, nc, has_init):
    it = iter(refs)
    r_ref, k_ref, v_ref, lw_ref, ag_ref, g_ref = (next(it) for _ in range(6))
    kk_ref, ka_ref, rk_ref, lg_ref, lb_ref = (next(it) for _ in range(5))
    s0_ref = next(it) if has_init else None
    z_ref, so_ref, st_ref = next(it), next(it), next(it)
    L2 = 2 * L
    pairs = range(npair)
    bf = lambda t: t.astype(BF16)

    @pl.when(pl.program_id(2) == 0)
    def _():
        if has_init:
            zero = jnp.zeros((HEAD, HEAD), F32)
            for p in pairs:
                top = jnp.concatenate([s0_ref[0, 2 * p], zero], axis=1)
                bot = jnp.concatenate([zero, s0_ref[0, 2 * p + 1]], axis=1)
                st_ref[p] = jnp.concatenate([top, bot], axis=0).T
        else:
            st_ref[...] = jnp.zeros(st_ref.shape, F32)

    def iota(shape, dim):
        return lax.broadcasted_iota(jnp.int32, shape, dim)

    tri = (iota((L, L), 0) >= iota((L, L), 1)).astype(BF16)
    lw = lw_ref[...]
    lw_hi, lw_lo = _split_bf16(lw)
    cum = _dot(tri, lw_hi) + _dot(tri, lw_lo)
    cum_last = cum[L - 1:L, :]
    e_pos = jnp.exp(cum)
    e_neg = jnp.exp(-cum)
    e_prev = jnp.exp(cum - lw)
    e_last = jnp.exp(cum_last - cum)
    lw_split = jnp.concatenate([lw_hi, lw_lo], axis=0)

    rows, cols = iota((L2, L2), 0), iota((L2, L2), 1)
    same_head = (rows >= L) == (cols >= L)
    lower_strict = same_head & (rows > cols)
    lower_incl = same_head & (rows >= cols)
    eye = (rows == cols).astype(F32)
    own_lanes = ((iota((L2, LANES), 0) >= L) == (iota((L2, LANES), 1) >= HEAD)).astype(F32)
    ones = jnp.ones((L2, LANES), BF16)
    dup = lambda t: jnp.concatenate([t, t], axis=0)
    stack = lambda t: dup(t) * own_lanes
    unstack = lambda t: t[:L] + t[L:]
    lanes = lambda p: slice(p * LANES, (p + 1) * LANES)
    left, right = slice(0, LANES), slice(LANES, 2 * LANES)

    ops = []
    for p in pairs:
        sl = lanes(p)
        ag = ag_ref[:, sl]
        kraw = k_ref[:, sl]
        kk = stack(kraw * kk_ref[:, sl])
        kk = kk / jnp.maximum(jnp.sqrt(jnp.sum(kk * kk, axis=-1, keepdims=True)), 1e-12)
        ks = stack(kraw * (1.0 + (ag - 1.0) * ka_ref[:, sl]))
        rs = stack(r_ref[:, sl])
        vs = stack(v_ref[:, sl])
        b = kk * dup(ag)
        en = dup(e_neg[:, sl])
        el = dup(e_last[:, sl])
        ops.append(dict(
            ks=ks, rs=rs, vs=vs, vb=bf(vs),
            at=bf(-kk * dup(e_prev[:, sl])), rt=rs * dup(e_pos[:, sl]),
            bt=bf(b * en), kt=bf(ks * en), bh=bf(b * el), kh=bf(ks * el)))

    for o in ops:
        o["rtb"] = bf(o["rt"])
        if L2 % LANES == 0:
            m = _dot_nt(jnp.concatenate([o["at"], o["rtb"]], axis=0),
                        jnp.concatenate([o["bt"], o["kt"]], axis=0))
            m_ab, m_ak, m_rb, m_rk = m[:L2, :L2], m[:L2, L2:], m[L2:, :L2], m[L2:, L2:]
        else:
            m_ab, m_ak = _dot_nt(o["at"], o["bt"]), _dot_nt(o["at"], o["kt"])
            m_rb, m_rk = _dot_nt(o["rtb"], o["bt"]), _dot_nt(o["rtb"], o["kt"])
        o["m_ab"] = jnp.where(lower_strict, m_ab, 0.0)
        o["m_ak"] = bf(jnp.where(lower_strict, m_ak, 0.0))
        o["m_rb"] = bf(jnp.where(lower_incl, m_rb, 0.0))
        o["m_rk"] = bf(jnp.where(lower_incl, m_rk, 0.0))

    for o in ops:
        o["tinv"] = eye + o["m_ab"]
        o["mp"] = o["m_ab"]
    span = 2
    while span < L:
        for o in ops:
            mpb = bf(o["mp"])
            o["mp"] = _dot(mpb, mpb)
        for o in ops:
            o["tinv"] = o["tinv"] + _dot(bf(o["tinv"]), bf(o["mp"]))
        span *= 2

    for o in ops:
        o["makv"] = bf(_dot(o["m_ak"], o["vb"]))
    for o in ops:
        au = _dot(bf(o["tinv"]), jnp.concatenate([o["at"], o["makv"]], axis=1))
        o["aub"] = bf(au)
    for o in ops:
        ry = _dot(o["m_rb"], o["aub"])
        o["rh"] = bf(o["rt"] + ry[:, left])
        o["y0"] = ry[:, right] + _dot(o["m_rk"], o["vb"])
        gq = _dot_tn(o["bh"], o["aub"])
        o["g"] = bf(gq[:, left])
        o["q"] = gq[:, right] + _dot_tn(o["kh"], o["vb"])
    last = pl.program_id(2) == nc - 1
    for p, o in zip(pairs, ops):
        decay = jnp.exp(_dot_tn(lw_split[:, lanes(p)], ones))
        st = st_ref[p]
        stb = bf(st)
        o["ys"] = _dot(o["rh"], stb) + o["y0"]
        st_ref[p] = decay * st + _dot(o["g"], stb) + o["q"]

    @pl.when(last)
    def _():
        for p in pairs:
            t = st_ref[p].T
            so_ref[0, 2 * p] = t[:HEAD, :HEAD]
            so_ref[0, 2 * p + 1] = t[HEAD:, HEAD:]

    for p, o in zip(pairs, ops):
        sl = lanes(p)
        ys = o["ys"]
        mean = jnp.sum(ys, axis=-1, keepdims=True) * (1.0 / HEAD)
        dev = (ys - mean) * own_lanes
        var = jnp.sum(dev * dev, axis=-1, keepdims=True) * (1.0 / HEAD)
        yn = unstack(dev * lax.rsqrt(var + GN_EPS)) * lg_ref[:, sl] + lb_ref[:, sl]
        bonus = unstack(jnp.sum(o["rs"] * o["ks"] * rk_ref[:, sl], axis=-1, keepdims=True) * o["vs"])
        z_ref[:, sl] = ((yn + bonus) * g_ref[:, sl]).astype(z_ref.dtype)


def _wkv(seqs, vecs, s0, *, name, n_seq, seq_len, row0, L, hw):
    d = seqs[0].shape[1]
    hw = min(hw, d)
    nc = seq_len // L
    blk0 = row0 // L
    npair = hw // LANES
    seq_spec = pl.BlockSpec((L, hw), lambda b, h, c: (blk0 + b * nc + c, h))
    vec_spec = pl.BlockSpec((1, hw), lambda b, h, c: (0, h))
    st_spec = pl.BlockSpec((1, 2 * npair, HEAD, HEAD), lambda b, h, c: (b, h, 0, 0))
    in_specs = [seq_spec] * 6 + [vec_spec] * 5 + ([st_spec] if s0 is not None else [])
    ins = list(seqs) + list(vecs) + ([s0] if s0 is not None else [])
    return pl.pallas_call(
        functools.partial(_wkv_kernel, L=L, npair=npair, nc=nc, has_init=s0 is not None),
        grid=(n_seq, d // hw, nc),
        in_specs=in_specs,
        out_specs=[pl.BlockSpec((L, hw), lambda b, h, c: (b * nc + c, h)), st_spec],
        out_shape=[jax.ShapeDtypeStruct((n_seq * seq_len, d), BF16),
                   jax.ShapeDtypeStruct((n_seq, d // HEAD, HEAD, HEAD), F32)],
        scratch_shapes=[pltpu.VMEM((npair, LANES, LANES), F32)],
        compiler_params=_params("parallel", "parallel", "arbitrary"),
        name=name,
    )(*ins)


def _head_masks():
    lane = lax.broadcasted_iota(jnp.int32, (1, LANES), 1)
    return lane < HEAD


def _attn_prompt_kernel(q_ref, k0, k1, k2, v0, v1, v2, bias_ref, o_ref, *, qb, npair):
    first = _head_masks()
    blk = pl.program_id(2)
    lanes = lambda p: slice(p * LANES, (p + 1) * LANES)
    cat = lambda refs, p: jnp.concatenate([r[:, lanes(p)] for r in refs], axis=0).astype(BF16)
    kpos = lax.broadcasted_iota(jnp.int32, (1, 3 * qb), 1) + (blk - 2) * qb
    valid = kpos >= 0
    units = []
    for p in range(npair):
        q = q_ref[:, lanes(p)]
        kcat = cat((k0, k1, k2), p)
        for h in range(2):
            qh = jnp.where(first if h == 0 else ~first, q, 0.0).astype(BF16)
            units.append(dict(p=p, h=2 * p + h, s=_dot_nt(qh, kcat)))
    for u in units:
        s = jnp.where(valid, u["s"] + bias_ref[u["h"]], NEG_INF)
        e = jnp.exp(s - jnp.max(s, axis=-1, keepdims=True))
        u["denom"] = jnp.sum(e, axis=-1, keepdims=True)
        u["e"] = e.astype(BF16)
    for p in range(npair):
        vcat = cat((v0, v1, v2), p)
        outs = [_dot(u["e"], vcat) / u["denom"] for u in units[2 * p:2 * p + 2]]
        o_ref[:, lanes(p)] = jnp.where(first, outs[0], outs[1]).astype(o_ref.dtype)


def _attn_prompt(q, k, v, bias, *, n_seq, seq_len, qb, hw=4 * LANES):
    d = q.shape[1]
    nq = seq_len // qb
    hw = min(hw, d)
    npair = hw // LANES
    q_spec = pl.BlockSpec((qb, hw), lambda h, b, i: (b * nq + i, h))
    kv = lambda off: pl.BlockSpec((qb, hw), lambda h, b, i: (b * nq + jnp.maximum(i - off, 0), h))
    return pl.pallas_call(
        functools.partial(_attn_prompt_kernel, qb=qb, npair=npair),
        grid=(d // hw, n_seq, nq),
        in_specs=[q_spec, kv(2), kv(1), kv(0), kv(2), kv(1), kv(0),
                  pl.BlockSpec((2 * npair, qb, 3 * qb), lambda h, b, i: (h, 0, 0))],
        out_specs=pl.BlockSpec((qb, hw), lambda h, b, i: (b * nq + i, h)),
        out_shape=jax.ShapeDtypeStruct((n_seq * seq_len, d), BF16),
        compiler_params=_params("parallel", "parallel", "arbitrary"),
        name="attn_prompt",
    )(q, k, k, k, v, v, v, bias)


def _attn_sample_kernel(q_ref, kn_ref, vn_ref, kc_ref, vc_ref, bc_ref, bn_ref, o_ref, *, npair):
    first = _head_masks()
    lanes = lambda p: slice(p * LANES, (p + 1) * LANES)
    units = []
    for p in range(npair):
        q = q_ref[:, lanes(p)]
        kc, kn = kc_ref[0, :, lanes(p)].astype(BF16), kn_ref[:, lanes(p)].astype(BF16)
        for h in range(2):
            qh = jnp.where(first if h == 0 else ~first, q, 0.0).astype(BF16)
            units.append(dict(p=p, h=2 * p + h, sc=_dot_nt(qh, kc), sn=_dot_nt(qh, kn)))
    for u in units:
        sc = u["sc"] + bc_ref[u["h"]]
        sn = u["sn"] + bn_ref[u["h"]]
        mx = jnp.maximum(jnp.max(sc, axis=-1, keepdims=True), jnp.max(sn, axis=-1, keepdims=True))
        pc, pn = jnp.exp(sc - mx), jnp.exp(sn - mx)
        u["denom"] = jnp.sum(pc, axis=-1, keepdims=True) + jnp.sum(pn, axis=-1, keepdims=True)
        u["pc"], u["pn"] = pc.astype(BF16), pn.astype(BF16)
    for u in units:
        vc, vn = vc_ref[0, :, lanes(u["p"])].astype(BF16), vn_ref[:, lanes(u["p"])].astype(BF16)
        u["o"] = (_dot(u["pc"], vc) + _dot(u["pn"], vn)) / u["denom"]
    for p in range(npair):
        o_ref[:, lanes(p)] = jnp.where(first, units[2 * p]["o"], units[2 * p + 1]["o"]).astype(o_ref.dtype)


def _attn_sample(q, k, v, cache_k, cache_v, bias_c, bias_n, *, n_seq, seq_len, row0):
    d = q.shape[1]
    kb = cache_k.shape[1]
    blk0 = row0 // seq_len
    new = pl.BlockSpec((seq_len, d), lambda s: (blk0 + s, 0))
    cache = pl.BlockSpec((1, kb, d), lambda s: (s, 0, 0))
    full = lambda a: pl.BlockSpec(a.shape, lambda s: (0,) * a.ndim)
    return pl.pallas_call(
        functools.partial(_attn_sample_kernel, npair=d // LANES),
        grid=(n_seq,),
        in_specs=[new, new, new, cache, cache, full(bias_c), full(bias_n)],
        out_specs=pl.BlockSpec((seq_len, d), lambda s: (s, 0)),
        out_shape=jax.ShapeDtypeStruct((n_seq * seq_len, d), BF16),
        compiler_params=_params("parallel"),
        name="attn_sample",
    )(q, k, v, cache_k, cache_v, bias_c, bias_n)


def _to_heads_kernel(*refs):
    n = len(refs) // 2
    for x_ref, o_ref in zip(refs[:n], refs[n:]):
        o_ref[...] = x_ref[...].reshape(o_ref.shape)


def _to_heads(arrs, *, row0, seq_stride, n_seq, rows):
    d = arrs[0].shape[1]
    tb = _tile(math.gcd(math.gcd(row0, seq_stride), rows), LANES, SUBLANES)
    in_spec = pl.BlockSpec((tb, d), lambda s, i: ((row0 + s * seq_stride) // tb + i, 0))
    out_spec = pl.BlockSpec((tb, d // HEAD, HEAD), lambda s, i: (s * (rows // tb) + i, 0, 0))
    return pl.pallas_call(
        _to_heads_kernel,
        grid=(n_seq, rows // tb),
        in_specs=[in_spec] * len(arrs),
        out_specs=[out_spec] * len(arrs),
        out_shape=[jax.ShapeDtypeStruct((n_seq * rows, d // HEAD, HEAD), a.dtype) for a in arrs],
        compiler_params=_params("parallel", "parallel"),
        name="to_heads",
    )(*arrs)


def _prompt_bias(table, qb):
    width = 3 * qb
    period = width + qb
    dist = (2 * qb - jnp.arange(period)) % period
    dist = jnp.where(dist > width, dist - period, dist)
    by_dist = table[:, jnp.clip(dist, -REL_CLIP, REL_CLIP) + REL_CLIP].astype(F32)
    heads = table.shape[0]
    rolled = jnp.tile(by_dist, (1, qb))[:, :qb * (period - 1)].reshape(heads, qb, period - 1)
    bias = rolled[:, :, :width]
    qi = jnp.arange(qb)[:, None]
    kj = jnp.arange(width)[None, :] - 2 * qb
    qc, kc = qi // CHUNK, jnp.floor_divide(kj, CHUNK)
    band = (kc <= qc) & (kc >= qc - PREV_CHUNKS)
    return jnp.where(band[None], bias, NEG_INF)


def _sample_bias(table, kb, t):
    dist = (kb + jnp.arange(t))[:, None] - jnp.arange(kb + t)[None, :]
    bias = table[:, jnp.clip(dist, -REL_CLIP, REL_CLIP) + REL_CLIP].astype(F32)
    return bias[:, :, :kb], bias[:, :, kb:]


def _swiglu_step(xb, w1, w3, w2):
    h1 = _dot(xb, w1)
    h3 = _dot(xb, w3)
    return _dot((h1 * jax.nn.sigmoid(h1) * h3).astype(BF16), w2)


def _dense_ffn_kernel(x_ref, w1_ref, w3_ref, w2_ref, g_ref, b_ref, o_ref, xb_ref, *, nf, alpha):
    f = pl.program_id(1)

    @pl.when(f == 0)
    def _():
        o_ref[...] = jnp.zeros(o_ref.shape, F32)
        xb_ref[...] = x_ref[...].astype(BF16)

    o_ref[...] += _swiglu_step(xb_ref[...], w1_ref[...].astype(BF16), w3_ref[...].astype(BF16),
                               w2_ref[...].astype(BF16))

    @pl.when(f == nf - 1)
    def _():
        o_ref[...] = _layer_norm(alpha * x_ref[...] + o_ref[...], g_ref[...], b_ref[...])


def _dense_ffn(x, w1, w3, w2, g, b, *, alpha, tm, tf=512):
    rows, d = x.shape
    dff = w1.shape[1]
    tf = _tile(dff, tf, LANES)
    nf = dff // tf
    row = pl.BlockSpec((tm, d), lambda t, f: (t, 0))
    vec = pl.BlockSpec((1, d), lambda t, f: (0, 0))
    w13 = pl.BlockSpec((d, tf), lambda t, f: (0, f))
    return pl.pallas_call(
        functools.partial(_dense_ffn_kernel, nf=nf, alpha=alpha),
        grid=(rows // tm, nf),
        in_specs=[row, w13, w13, pl.BlockSpec((tf, d), lambda t, f: (f, 0)), vec, vec],
        out_specs=row,
        out_shape=jax.ShapeDtypeStruct((rows, d), F32),
        scratch_shapes=[pltpu.VMEM((tm, d), BF16)],
        compiler_params=_params("parallel", "arbitrary"),
        name="dense_ffn",
    )(x, w1.astype(BF16), w3.astype(BF16), w2.astype(BF16), g, b)


def _router_kernel(x_ref, w_ref, o_ref, *, n_exp):
    x_hi, x_lo = _split_bf16(x_ref[...])
    w_hi, w_lo = _split_bf16(w_ref[...])
    logits = _dot(x_hi, w_hi) + _dot(x_hi, w_lo) + _dot(x_lo, w_hi)
    lane = lax.broadcasted_iota(jnp.int32, logits.shape, 1).astype(F32)
    valid = lane < n_exp
    lg = jnp.where(valid, logits, NEG_INF)
    ex = jnp.where(valid, jnp.exp(lg - jnp.max(lg, axis=-1, keepdims=True)), 0.0)
    probs = jnp.where(valid, ex / jnp.sum(ex, axis=-1, keepdims=True), -1.0)
    p1 = jnp.max(probs, axis=-1, keepdims=True)
    i1 = jnp.min(jnp.where(probs == p1, lane, float(LANES)), axis=-1, keepdims=True)
    rest = jnp.where(lane == i1, -1.0, probs)
    p2 = jnp.max(rest, axis=-1, keepdims=True)
    i2 = jnp.min(jnp.where(rest == p2, lane, float(LANES)), axis=-1, keepdims=True)
    tot = p1 + p2
    out = jnp.where(lane == 0, i1,
                    jnp.where(lane == 1, i2,
                              jnp.where(lane == 2, p1 / tot, jnp.where(lane == 3, p2 / tot, 0.0))))
    o_ref[...] = out


def _router(x, w, *, tm=512):
    m, d = x.shape
    n_exp = w.shape[1]
    wp = jnp.zeros((d, LANES), F32).at[:, :n_exp].set(w)
    tm = _tile(m, tm)
    return pl.pallas_call(
        functools.partial(_router_kernel, n_exp=n_exp),
        grid=(m // tm,),
        in_specs=[pl.BlockSpec((tm, d), lambda i: (i, 0)), pl.BlockSpec((d, LANES), lambda i: (0, 0))],
        out_specs=pl.BlockSpec((tm, LANES), lambda i: (i, 0)),
        out_shape=jax.ShapeDtypeStruct((m, LANES), F32),
        compiler_params=_params("parallel"),
        name="router",
    )(x, wp)


DMA_UNROLL = 8


def _row_copy(src_ref, dst_ref, sem, src_row, dst_row):
    return pltpu.make_async_copy(src_ref.at[pl.ds(src_row, 1)], dst_ref.at[pl.ds(dst_row, 1)], sem)


def _moe_ffn_kernel(te_ref, rows_ref, na_ref, src_ref, x_hbm, w1_ref, w3_ref, w2_ref, o_ref,
                    gbuf, xb_ref, sem, *, n_tiles, tm, sub, n_issue):
    t, f = pl.program_id(0), pl.program_id(1)
    rows = rows_ref[t]
    chunk = tm // n_issue

    def issue(tile, lo, n):
        def body(i, c):
            _row_copy(x_hbm, gbuf, sem, src_ref[tile * tm + lo + i], lo + i).start()
            return c
        lax.fori_loop(0, n, body, 0, unroll=DMA_UNROLL)

    def wait_tile():
        def body(i, c):
            _row_copy(x_hbm, gbuf, sem, 0, i).wait()
            return c
        lax.fori_loop(0, tm, body, 0, unroll=DMA_UNROLL)

    @pl.when(jnp.logical_and(jnp.logical_and(t == 0, f == 0), rows > 0))
    def _():
        issue(0, 0, tm)

    @pl.when(f == 0)
    def _():
        o_ref[...] = jnp.zeros(o_ref.shape, F32)

        @pl.when(rows > 0)
        def _():
            wait_tile()
            xb_ref[...] = gbuf[...].astype(BF16)

    nxt = jnp.minimum(t + 1, n_tiles - 1)
    fetch_next = jnp.logical_and(jnp.logical_and(f >= 1, f <= n_issue),
                                 jnp.logical_and(t + 1 < n_tiles, rows_ref[nxt] > 0))

    @pl.when(fetch_next)
    def _():
        issue(t + 1, (f - 1) * chunk, chunk)

    for n in range(sub, tm + 1, sub):
        @pl.when(jnp.logical_and(rows > n - sub, rows <= n))
        def _():
            o_ref[:n, :] += _swiglu_step(xb_ref[:n, :], w1_ref[...].astype(BF16),
                                         w3_ref[...].astype(BF16), w2_ref[...].astype(BF16))


def _moe_ffn(x, src, w1, w3, w2, layer, tile_expert, tile_rows, n_active, *, tm, tf):
    d = x.shape[1]
    n_slots = src.shape[0]
    n_tiles = n_slots // tm
    dff = w1.shape[3]
    tf = _tile(dff, min(tf, dff // 2), LANES)
    nf = dff // tf
    n_issue = 1
    while 2 * n_issue <= max(nf - 1, 1) and tm % (2 * n_issue) == 0:
        n_issue *= 2
    assert nf >= 2
    sub = min(tm, 256)

    def clamp(t, na):
        return jnp.minimum(t, na[0] - 1)

    def f_idx(t, f, na):
        return jnp.where(t < na[0], f, nf - 1)

    w13 = pl.BlockSpec((None, None, d, tf),
                       lambda t, f, te, tr, na, sr: (layer, te[clamp(t, na)], 0, f_idx(t, f, na)))
    w2s = pl.BlockSpec((None, None, tf, d),
                       lambda t, f, te, tr, na, sr: (layer, te[clamp(t, na)], f_idx(t, f, na), 0))
    return pl.pallas_call(
        functools.partial(_moe_ffn_kernel, n_tiles=n_tiles, tm=tm, sub=sub, n_issue=n_issue),
        grid_spec=pltpu.PrefetchScalarGridSpec(
            num_scalar_prefetch=4,
            grid=(n_tiles, nf),
            in_specs=[pl.BlockSpec(memory_space=pl.ANY), w13, w13, w2s],
            out_specs=pl.BlockSpec((tm, d), lambda t, f, te, tr, na, sr: (t, 0)),
            scratch_shapes=[pltpu.VMEM((tm, d), F32), pltpu.VMEM((tm, d), BF16),
                            pltpu.SemaphoreType.DMA(())]),
        out_shape=jax.ShapeDtypeStruct((n_slots, d), F32),
        compiler_params=_params("arbitrary", "arbitrary"),
        name="moe_ffn",
    )(tile_expert, tile_rows, n_active, src, x, w1, w3, w2)


def _combine_kernel(i0_ref, i1_ref, y_ref, x_ref, sel_ref, g_ref, b_ref, o_ref, buf, sem, *, tc, alpha):
    base = pl.program_id(0) * tc

    def start(i, c):
        _row_copy(y_ref, buf.at[0], sem, i0_ref[base + i], i).start()
        _row_copy(y_ref, buf.at[1], sem, i1_ref[base + i], i).start()
        return c

    def wait(i, c):
        _row_copy(y_ref, buf.at[0], sem, 0, i).wait()
        _row_copy(y_ref, buf.at[1], sem, 0, i).wait()
        return c

    lax.fori_loop(0, tc, start, 0, unroll=DMA_UNROLL)
    lax.fori_loop(0, tc, wait, 0, unroll=DMA_UNROLL)
    sel = sel_ref[...]
    mixed = buf[0] * sel[:, TOP_K:TOP_K + 1] + buf[1] * sel[:, TOP_K + 1:TOP_K + 2]
    o_ref[...] = _layer_norm(alpha * x_ref[...] + mixed, g_ref[...], b_ref[...])


def _combine(y, x, sel, i0, i1, g, b, *, alpha, tc=256):
    m, d = x.shape
    tc = _tile(m, tc)
    row = pl.BlockSpec((tc, d), lambda i, a, c: (i, 0))
    vec = pl.BlockSpec((1, d), lambda i, a, c: (0, 0))
    return pl.pallas_call(
        functools.partial(_combine_kernel, tc=tc, alpha=alpha),
        grid_spec=pltpu.PrefetchScalarGridSpec(
            num_scalar_prefetch=2,
            grid=(m // tc,),
            in_specs=[pl.BlockSpec(memory_space=pl.ANY), row,
                      pl.BlockSpec((tc, LANES), lambda i, a, c: (i, 0)), vec, vec],
            out_specs=row,
            scratch_shapes=[pltpu.VMEM((2, tc, d), F32), pltpu.SemaphoreType.DMA(())]),
        out_shape=jax.ShapeDtypeStruct((m, d), F32),
        compiler_params=_params("arbitrary"),
        name="moe_combine",
    )(i0, i1, y, x, sel, g, b)


def _moe(x, router, w1, w3, w2, layer, g, b, *, alpha, tm):
    m, d = x.shape
    n_exp = w1.shape[1]
    sel = _router(x, router)
    top_i = sel[:, :TOP_K].astype(jnp.int32)
    e_flat = top_i.reshape(-1)
    onehot = (e_flat[:, None] == jnp.arange(n_exp)[None, :]).astype(jnp.int32)
    rank = jnp.sum((jnp.cumsum(onehot, axis=0) - onehot) * onehot, axis=1)
    counts = jnp.sum(onehot, axis=0)
    padded = (counts + tm - 1) // tm * tm
    ends = jnp.cumsum(padded)
    slot = jnp.sum(onehot * (ends - padded)[None, :], axis=1) + rank
    n_slots = (TOP_K * m + n_exp * (tm - 1)) // tm * tm
    n_tiles = n_slots // tm
    src = jnp.zeros((n_slots,), jnp.int32).at[slot].set(jnp.arange(TOP_K * m, dtype=jnp.int32) // TOP_K)
    tile_start = jnp.arange(n_tiles, dtype=jnp.int32) * tm
    tile_expert = jnp.minimum(jnp.sum((ends[None, :] <= tile_start[:, None]).astype(jnp.int32), axis=1),
                              n_exp - 1).astype(jnp.int32)
    group_fill = (ends - padded + counts).astype(jnp.int32)
    tile_rows = jnp.clip(group_fill[tile_expert] - tile_start, 0, tm).astype(jnp.int32)
    n_active = (ends[-1:] // tm).astype(jnp.int32)

    ys = _moe_ffn(x, src, w1, w3, w2, layer, tile_expert, tile_rows, n_active, tm=tm, tf=512)
    slot2 = slot.reshape(m, TOP_K).astype(jnp.int32)
    return _combine(ys, x, sel, slot2[:, 0], slot2[:, 1], g, b, alpha=alpha)


def kernel(x_prompt, x_sample, state_wkv, state_shift, cache_k, cache_v, ln_g, ln_b, rwkv_mu, rwkv_wr, rwkv_wk, rwkv_wv, rwkv_wo, rwkv_w0, rwkv_w1, rwkv_w2, rwkv_a0, rwkv_a1, rwkv_a2, rwkv_v0, rwkv_v1, rwkv_v2, rwkv_g1, rwkv_g2, rwkv_k_k, rwkv_k_a, rwkv_r_k, rwkv_lnx_g, rwkv_lnx_b, attn_wk, attn_wv, attn_wq, attn_wo, attn_rel_bias, ffn_w1, ffn_w3, ffn_w2, moe_router, moe_w1, moe_w3, moe_w2):
    bp, tp, d = x_prompt.shape
    bs, ts, _ = x_sample.shape
    depth = ln_g.shape[0]
    n_a = rwkv_wr.shape[0]
    alpha = (2 * depth) ** 0.25
    mp, ms = bp * tp, bs * ts
    m = mp + ms
    kb = cache_k.shape[1]
    assert tp % CHUNK == 0 and d % LANES == 0 and mp % ts == 0
    row = lambda a: a.reshape(1, -1)
    bf = lambda a: a.astype(BF16)
    qb = _tile(tp, 256, CHUNK)
    assert 2 * qb >= PREV
    tm_dense = _tile(m, 640)
    tm_moe = 1024 if m >= 4096 else 64
    tm_pre = _tile(math.gcd(mp, ms), 256)

    x = jnp.concatenate([x_prompt.reshape(mp, d), x_sample.reshape(ms, d)], axis=0)
    ridx = jnp.arange(m)
    start_flag = jnp.where(ridx < mp, (ridx % tp == 0) * 1.0, ((ridx - mp) % ts == 0) * 2.0)
    start_flag = start_flag.astype(F32).reshape(m, 1)

    p_wkv, s_wkv, p_shift, s_shift = [], [], [], []
    v_first = k_sh = v_sh = None
    for l in range(depth):
        g0, b0, g1, b1 = row(ln_g[l, 0]), row(ln_b[l, 0]), row(ln_g[l, 1]), row(ln_b[l, 1])
        if l < n_a:
            p_shift.append(x[tp - 1:mp:tp])
            s_shift.append(x[mp + ts - 1::ts])
            start_rows = jnp.repeat(state_shift[l], ts, axis=0)
            has_v = l > 0
            lora1 = [bf(rwkv_w1[l]), bf(rwkv_a1[l]), bf(rwkv_g1[l])] + ([bf(rwkv_v1[l - 1])] if has_v else [])
            lora2 = [bf(rwkv_w2[l]), bf(rwkv_a2[l]), bf(rwkv_g2[l])] + ([bf(rwkv_v2[l - 1])] if has_v else [])
            bias = [row(rwkv_w0[l]), row(rwkv_a0[l])] + ([row(rwkv_v0[l - 1])] if has_v else [])
            pre = _rwkv_pre(x, start_rows, start_flag, rwkv_mu[l], lora1, lora2, bias, tm=tm_pre)
            xr, xk, xv, lw, ag, gg = pre[:6]
            r = _matmul(xr, rwkv_wr, layer=l, name="rwkv_r")
            k = _matmul(xk, rwkv_wk, layer=l, name="rwkv_k")
            if has_v:
                v = _matmul(xv, rwkv_wv, layer=l, name="rwkv_v", extras=(v_first, pre[6]),
                            epilogue=lambda acc, vf, vg: acc + (vf - acc) * vg)
            else:
                v = _matmul(xv, rwkv_wv, layer=l, name="rwkv_v")
                v_first = v
            seqs = (r, k, v, lw, ag, gg)
            vecs = [row(rwkv_k_k[l]), row(rwkv_k_a[l]), row(rwkv_r_k[l]), row(rwkv_lnx_g[l]), row(rwkv_lnx_b[l])]
            zp, stp = _wkv(seqs, vecs, None, name="wkv_prompt", n_seq=bp, seq_len=tp, row0=0, L=CHUNK, hw=2048)
            zs, sts = _wkv(seqs, vecs, state_wkv[l].astype(F32), name="wkv_sample",
                           n_seq=bs, seq_len=ts, row0=mp, L=ts, hw=2048)
            p_wkv.append(stp.astype(state_wkv.dtype))
            s_wkv.append(sts.astype(state_wkv.dtype))
            z = jnp.concatenate([zp, zs], axis=0)
            x = _matmul(z, bf(rwkv_wo[l]), name="rwkv_out_ln", extras=(x,), vecs=(g0, b0),
                        epilogue=_ln_epilogue(alpha), tm=256, tn=d)
        else:
            i = l - n_a
            q = _matmul(x, attn_wq, layer=i, name="attn_q", out_dtype=BF16,
                        epilogue=lambda acc: acc * (HEAD ** -0.5))
            table = attn_rel_bias[i]
            op = _attn_prompt(q, k_sh, v_sh, _prompt_bias(table, qb), n_seq=bp, seq_len=tp, qb=qb)
            bias_c, bias_n = _sample_bias(table, kb, ts)
            os_ = _attn_sample(q, k_sh, v_sh, cache_k.reshape(bs, kb, d), cache_v.reshape(bs, kb, d),
                               bias_c, bias_n, n_seq=bs, seq_len=ts, row0=mp)
            o = jnp.concatenate([op, os_], axis=0)
            x = _matmul(o, bf(attn_wo[i]), name="attn_out_ln", extras=(x,), vecs=(g0, b0),
                        epilogue=_ln_epilogue(alpha), tm=256, tn=d)
        j = l // 2
        if l % 2 == 0:
            x = _dense_ffn(x, ffn_w1[j], ffn_w3[j], ffn_w2[j], g1, b1, alpha=alpha, tm=tm_dense)
        else:
            x = _moe(x, moe_router[j], moe_w1, moe_w3, moe_w2, j, g1, b1, alpha=alpha, tm=tm_moe)
        if l == n_a - 1:
            k_sh = _matmul(x, attn_wk, name="attn_k")
            v_sh = _matmul(x, attn_wv, name="attn_v")

    heads = d // HEAD
    keep = min(PREV, tp)
    kp, vp = _to_heads((k_sh, v_sh), row0=tp - keep, seq_stride=tp, n_seq=bp, rows=keep)
    ks, vs = _to_heads((k_sh, v_sh), row0=mp, seq_stride=ms, n_seq=1, rows=ms)
    return (x[:mp].reshape(bp, tp, d), x[mp:].reshape(bs, ts, d),
            jnp.stack(p_wkv), jnp.stack(p_shift),
            kp.reshape(bp, keep, heads, HEAD), vp.reshape(bp, keep, heads, HEAD),
            jnp.stack(s_wkv), jnp.stack(s_shift),
            ks.reshape(bs, ts, heads, HEAD), vs.reshape(bs, ts, heads, HEAD))
```

```python
import functools
import math

import jax
import jax.numpy as jnp
from jax import lax
from jax.experimental import pallas as pl
from jax.experimental.pallas import tpu as pltpu

F32 = jnp.float32
BF16 = jnp.bfloat16

CHUNK = 64
PREV_CHUNKS = 8
PREV = PREV_CHUNKS * CHUNK
REL_CLIP = 128
HEAD = 64
TOP_K = 2
GN_EPS = 64e-5
LN_EPS = 1e-5
NEG_INF = -1e30

LANES = 128
SUBLANES = 8
VMEM_LIMIT_BYTES = 60 * 1024 * 1024
MOE_TILE_MAX = 1152


def _params(*sem):
    return pltpu.CompilerParams(dimension_semantics=sem, vmem_limit_bytes=VMEM_LIMIT_BYTES)


def _tile(n, target, mult=16):
    best = None
    for t in range(mult, min(n, target) + 1, mult):
        if n % t == 0:
            best = t
    assert best is not None, (n, target, mult)
    return best


def _dot(a, b):
    return jnp.dot(a, b, preferred_element_type=F32)


def _dot_nt(a, b):
    return lax.dot_general(a, b, (((1,), (1,)), ((), ())), preferred_element_type=F32)


def _dot_tn(a, b):
    return lax.dot_general(a, b, (((0,), (0,)), ((), ())), preferred_element_type=F32)


def _split_bf16(x):
    hi = x.astype(BF16)
    lo = (x - hi.astype(F32)).astype(BF16)
    return hi, lo


def _layer_norm(y, g, b):
    mu = jnp.mean(y, axis=-1, keepdims=True)
    dev = y - mu
    var = jnp.mean(dev * dev, axis=-1, keepdims=True)
    return dev * lax.rsqrt(var + LN_EPS) * g + b


def _softplus(u):
    return jnp.maximum(u, 0.0) + jnp.log(1.0 + jnp.exp(-jnp.abs(u)))


def _mm_kernel(*refs, n_extra, n_vec, epilogue, cast_w):
    x_ref, w_ref = refs[0], refs[1]
    extra = refs[2:2 + n_extra]
    vecs = refs[2 + n_extra:2 + n_extra + n_vec]
    o_ref = refs[2 + n_extra + n_vec]
    if cast_w:
        wb_ref = refs[-1]

        @pl.when(pl.program_id(1) == 0)
        def _():
            wb_ref[...] = w_ref[...].astype(BF16)

        w = wb_ref[...]
    else:
        w = w_ref[...]
    acc = _dot(x_ref[...].astype(BF16), w)
    if epilogue is not None:
        acc = epilogue(acc, *[e[...] for e in extra], *[v[...] for v in vecs])
    o_ref[...] = acc.astype(o_ref.dtype)


def _matmul(x, w, *, name, layer=None, out_dtype=F32, epilogue=None, extras=(), vecs=(), tm=512, tn=1024):
    m, k = x.shape
    n = w.shape[-1]
    tm = _tile(m, tm)
    tn = n if n <= tn else _tile(n, tn, LANES)
    cast_w = w.dtype != BF16
    if w.ndim == 3:
        w_spec = pl.BlockSpec((None, k, tn), lambda j, i: (layer, 0, j))
    else:
        w_spec = pl.BlockSpec((k, tn), lambda j, i: (0, j))
    in_specs = [pl.BlockSpec((tm, k), lambda j, i: (i, 0)), w_spec]
    in_specs += [pl.BlockSpec((tm, tn), lambda j, i: (i, j)) for _ in extras]
    in_specs += [pl.BlockSpec((1, tn), lambda j, i: (0, j)) for _ in vecs]
    return pl.pallas_call(
        functools.partial(_mm_kernel, n_extra=len(extras), n_vec=len(vecs),
                          epilogue=epilogue, cast_w=cast_w),
        grid=(n // tn, m // tm),
        in_specs=in_specs,
        out_specs=pl.BlockSpec((tm, tn), lambda j, i: (i, j)),
        out_shape=jax.ShapeDtypeStruct((m, n), out_dtype),
        scratch_shapes=[pltpu.VMEM((k, tn), BF16)] if cast_w else [],
        compiler_params=_params("parallel", "arbitrary"),
        name=name,
    )(x, w, *extras, *vecs)


def _ln_epilogue(alpha):
    def epi(acc, xres, g, b):
        return _layer_norm(alpha * xres + acc, g, b)
    return epi


def _rwkv_pre_kernel(*refs, has_v):
    it = iter(refs)
    x_ref, halo_ref, start_ref, flag_ref, mu_ref = (next(it) for _ in range(5))
    w1, a1, g1 = next(it), next(it), next(it)
    v1 = next(it) if has_v else None
    w2, a2, g2 = next(it), next(it), next(it)
    v2 = next(it) if has_v else None
    w0, a0 = next(it), next(it)
    v0 = next(it) if has_v else None
    xr_ref, xk_ref, xv_ref, lw_ref, ag_ref, g_ref = (next(it) for _ in range(6))
    vg_ref = next(it) if has_v else None

    x = x_ref[...]
    row = lax.broadcasted_iota(jnp.int32, (x.shape[0], 1), 0)
    prev = jnp.where(row == 0, halo_ref[SUBLANES - 1:SUBLANES, :], pltpu.roll(x, 1, 0))
    flag = flag_ref[...]
    prev = jnp.where(flag == 1.0, 0.0, jnp.where(flag == 2.0, start_ref[...], prev))
    xx = prev - x
    mix = lambda i: (x + xx * mu_ref[i:i + 1, :]).astype(BF16)
    xr_ref[...] = mix(0)
    xk_ref[...] = mix(2)
    xv = mix(3)
    xv_ref[...] = xv
    hw = jnp.tanh(_dot(mix(1), w1[...]))
    z = w0[...] + _dot(hw.astype(BF16), w2[...])
    lw_ref[...] = -jnp.exp(-_softplus(-z) - 0.5)
    ha = _dot(mix(4), a1[...])
    ag_ref[...] = jax.nn.sigmoid(a0[...] + _dot(ha.astype(BF16), a2[...]))
    hg = jax.nn.sigmoid(_dot(mix(5), g1[...]))
    g_ref[...] = _dot(hg.astype(BF16), g2[...])
    if has_v:
        hv = _dot(xv, v1[...])
        vg_ref[...] = jax.nn.sigmoid(v0[...] + _dot(hv.astype(BF16), v2[...]))


def _rwkv_pre(x, start_rows, flag, mu, lora1, lora2, bias, *, tm):
    m, d = x.shape
    first_start_tile = (m - start_rows.shape[0]) // tm
    has_v = len(lora1) == 4
    row = pl.BlockSpec((tm, d), lambda i: (i, 0))
    halo = pl.BlockSpec((SUBLANES, d), lambda i: (jnp.maximum(i * (tm // SUBLANES) - 1, 0), 0))
    start = pl.BlockSpec((tm, d), lambda i: (jnp.maximum(i - first_start_tile, 0), 0))
    full = lambda a: pl.BlockSpec(a.shape, lambda i: (0,) * a.ndim)
    ins = [x, x, start_rows, flag, mu, *lora1, *lora2, *bias]
    in_specs = [row, halo, start, pl.BlockSpec((tm, 1), lambda i: (i, 0))] + [full(a) for a in ins[4:]]
    out_shape = [jax.ShapeDtypeStruct((m, d), BF16)] * 3 + [jax.ShapeDtypeStruct((m, d), F32)] * (3 + has_v)
    return pl.pallas_call(
        functools.partial(_rwkv_pre_kernel, has_v=has_v),
        grid=(m // tm,),
        in_specs=in_specs,
        out_specs=[row] * len(out_shape),
        out_shape=out_shape,
        compiler_params=_params("parallel"),
        name="rwkv_pre",
    )(*ins)


def _wkv_kernel(*refs, L, npair, nc, has_init):
    it = iter(refs)
    r_ref, k_ref, v_ref, lw_ref, ag_ref, g_ref = (next(it) for _ in range(6))
    kk_ref, ka_ref, rk_ref, lg_ref, lb_ref = (next(it) for _ in range(5))
    s0_ref = next(it) if has_init else None
    z_ref, so_ref, st_ref = next(it), next(it), next(it)
    L2 = 2 * L
    pairs = range(npair)
    bf = lambda t: t.astype(BF16)

    @pl.when(pl.program_id(2) == 0)
    def _():
        if has_init:
            zero = jnp.zeros((HEAD, HEAD), F32)
            for p in pairs:
                top = jnp.concatenate([s0_ref[0, 2 * p], zero], axis=1)
                bot = jnp.concatenate([zero, s0_ref[0, 2 * p + 1]], axis=1)
                st_ref[p] = jnp.concatenate([top, bot], axis=0).T
        else:
            st_ref[...] = jnp.zeros(st_ref.shape, F32)

    def iota(shape, dim):
        return lax.broadcasted_iota(jnp.int32, shape, dim)

    tri = (iota((L, L), 0) >= iota((L, L), 1)).astype(BF16)
    lw = lw_ref[...]
    lw_hi, lw_lo = _split_bf16(lw)
    cum = _dot(tri, lw_hi) + _dot(tri, lw_lo)
    cum_last = cum[L - 1:L, :]
    e_pos = jnp.exp(cum)
    e_neg = jnp.exp(-cum)
    e_prev = jnp.exp(cum - lw)
    e_last = jnp.exp(cum_last - cum)
    lw_split = jnp.concatenate([lw_hi, lw_lo], axis=0)

    rows, cols = iota((L2, L2), 0), iota((L2, L2), 1)
    same_head = (rows >= L) == (cols >= L)
    lower_strict = same_head & (rows > cols)
    lower_incl = same_head & (rows >= cols)
    eye = (rows == cols).astype(F32)
    own_lanes = ((iota((L2, LANES), 0) >= L) == (iota((L2, LANES), 1) >= HEAD)).astype(F32)
    ones = jnp.ones((L2, LANES), BF16)
    dup = lambda t: jnp.concatenate([t, t], axis=0)
    stack = lambda t: dup(t) * own_lanes
    unstack = lambda t: t[:L] + t[L:]
    lanes = lambda p: slice(p * LANES, (p + 1) * LANES)
    left, right = slice(0, LANES), slice(LANES, 2 * LANES)

    ops = []
    for p in pairs:
        sl = lanes(p)
        ag = ag_ref[:, sl]
        kraw = k_ref[:, sl]
        kk = stack(kraw * kk_ref[:, sl])
        kk = kk / jnp.maximum(jnp.sqrt(jnp.sum(kk * kk, axis=-1, keepdims=True)), 1e-12)
        ks = stack(kraw * (1.0 + (ag - 1.0) * ka_ref[:, sl]))
        rs = stack(r_ref[:, sl])
        vs = stack(v_ref[:, sl])
        b = kk * dup(ag)
        en = dup(e_neg[:, sl])
        el = dup(e_last[:, sl])
        ops.append(dict(
            ks=ks, rs=rs, vs=vs, vb=bf(vs),
            at=bf(-kk * dup(e_prev[:, sl])), rt=rs * dup(e_pos[:, sl]),
            bt=bf(b * en), kt=bf(ks * en), bh=bf(b * el), kh=bf(ks * el)))

    for o in ops:
        o["rtb"] = bf(o["rt"])
        if L2 % LANES == 0:
            m = _dot_nt(jnp.concatenate([o["at"], o["rtb"]], axis=0),
                        jnp.concatenate([o["bt"], o["kt"]], axis=0))
            m_ab, m_ak, m_rb, m_rk = m[:L2, :L2], m[:L2, L2:], m[L2:, :L2], m[L2:, L2:]
        else:
            m_ab, m_ak = _dot_nt(o["at"], o["bt"]), _dot_nt(o["at"], o["kt"])
            m_rb, m_rk = _dot_nt(o["rtb"], o["bt"]), _dot_nt(o["rtb"], o["kt"])
        o["m_ab"] = jnp.where(lower_strict, m_ab, 0.0)
        o["m_ak"] = bf(jnp.where(lower_strict, m_ak, 0.0))
        o["m_rb"] = bf(jnp.where(lower_incl, m_rb, 0.0))
        o["m_rk"] = bf(jnp.where(lower_incl, m_rk, 0.0))

    for o in ops:
        o["tinv"] = eye + o["m_ab"]
        o["mp"] = o["m_ab"]
    span = 2
    while span < L:
        for o in ops:
            mpb = bf(o["mp"])
            o["mp"] = _dot(mpb, mpb)
        for o in ops:
            o["tinv"] = o["tinv"] + _dot(bf(o["tinv"]), bf(o["mp"]))
        span *= 2

    for o in ops:
        o["makv"] = bf(_dot(o["m_ak"], o["vb"]))
    for o in ops:
        au = _dot(bf(o["tinv"]), jnp.concatenate([o["at"], o["makv"]], axis=1))
        o["aub"] = bf(au)
    for o in ops:
        ry = _dot(o["m_rb"], o["aub"])
        o["rh"] = bf(o["rt"] + ry[:, left])
        o["y0"] = ry[:, right] + _dot(o["m_rk"], o["vb"])
        gq = _dot_tn(o["bh"], o["aub"])
        o["g"] = bf(gq[:, left])
        o["q"] = gq[:, right] + _dot_tn(o["kh"], o["vb"])
    last = pl.program_id(2) == nc - 1
    for p, o in zip(pairs, ops):
        decay = jnp.exp(_dot_tn(lw_split[:, lanes(p)], ones))
        st = st_ref[p]
        stb = bf(st)
        o["ys"] = _dot(o["rh"], stb) + o["y0"]
        st_ref[p] = decay * st + _dot(o["g"], stb) + o["q"]

    @pl.when(last)
    def _():
        for p in pairs:
            t = st_ref[p].T
            so_ref[0, 2 * p] = t[:HEAD, :HEAD]
            so_ref[0, 2 * p + 1] = t[HEAD:, HEAD:]

    for p, o in zip(pairs, ops):
        sl = lanes(p)
        ys = o["ys"]
        mean = jnp.sum(ys, axis=-1, keepdims=True) * (1.0 / HEAD)
        dev = (ys - mean) * own_lanes
        var = jnp.sum(dev * dev, axis=-1, keepdims=True) * (1.0 / HEAD)
        yn = unstack(dev * lax.rsqrt(var + GN_EPS)) * lg_ref[:, sl] + lb_ref[:, sl]
        bonus = unstack(jnp.sum(o["rs"] * o["ks"] * rk_ref[:, sl], axis=-1, keepdims=True) * o["vs"])
        z_ref[:, sl] = ((yn + bonus) * g_ref[:, sl]).astype(z_ref.dtype)


def _wkv(seqs, vecs, s0, *, name, n_seq, seq_len, row0, L, hw):
    d = seqs[0].shape[1]
    hw = min(hw, d)
    nc = seq_len // L
    blk0 = row0 // L
    npair = hw // LANES
    seq_spec = pl.BlockSpec((L, hw), lambda b, h, c: (blk0 + b * nc + c, h))
    vec_spec = pl.BlockSpec((1, hw), lambda b, h, c: (0, h))
    st_spec = pl.BlockSpec((1, 2 * npair, HEAD, HEAD), lambda b, h, c: (b, h, 0, 0))
    in_specs = [seq_spec] * 6 + [vec_spec] * 5 + ([st_spec] if s0 is not None else [])
    ins = list(seqs) + list(vecs) + ([s0] if s0 is not None else [])
    return pl.pallas_call(
        functools.partial(_wkv_kernel, L=L, npair=npair, nc=nc, has_init=s0 is not None),
        grid=(n_seq, d // hw, nc),
        in_specs=in_specs,
        out_specs=[pl.BlockSpec((L, hw), lambda b, h, c: (b * nc + c, h)), st_spec],
        out_shape=[jax.ShapeDtypeStruct((n_seq * seq_len, d), BF16),
                   jax.ShapeDtypeStruct((n_seq, d // HEAD, HEAD, HEAD), F32)],
        scratch_shapes=[pltpu.VMEM((npair, LANES, LANES), F32)],
        compiler_params=_params("parallel", "parallel", "arbitrary"),
        name=name,
    )(*ins)


def _head_masks():
    lane = lax.broadcasted_iota(jnp.int32, (1, LANES), 1)
    return lane < HEAD


def _attn_prompt_kernel(q_ref, k0, k1, k2, v0, v1, v2, bias_ref, o_ref, *, qb, npair):
    first = _head_masks()
    blk = pl.program_id(2)
    lanes = lambda p: slice(p * LANES, (p + 1) * LANES)
    cat = lambda refs, p: jnp.concatenate([r[:, lanes(p)] for r in refs], axis=0).astype(BF16)
    kpos = lax.broadcasted_iota(jnp.int32, (1, 3 * qb), 1) + (blk - 2) * qb
    valid = kpos >= 0
    units = []
    for p in range(npair):
        q = q_ref[:, lanes(p)]
        kcat = cat((k0, k1, k2), p)
        for h in range(2):
            qh = jnp.where(first if h == 0 else ~first, q, 0.0).astype(BF16)
            units.append(dict(p=p, h=2 * p + h, s=_dot_nt(qh, kcat)))
    for u in units:
        s = jnp.where(valid, u["s"] + bias_ref[u["h"]], NEG_INF)
        e = jnp.exp(s - jnp.max(s, axis=-1, keepdims=True))
        u["denom"] = jnp.sum(e, axis=-1, keepdims=True)
        u["e"] = e.astype(BF16)
    for p in range(npair):
        vcat = cat((v0, v1, v2), p)
        outs = [_dot(u["e"], vcat) / u["denom"] for u in units[2 * p:2 * p + 2]]
        o_ref[:, lanes(p)] = jnp.where(first, outs[0], outs[1]).astype(o_ref.dtype)


def _attn_prompt(q, k, v, bias, *, n_seq, seq_len, qb, hw=4 * LANES):
    d = q.shape[1]
    nq = seq_len // qb
    hw = min(hw, d)
    npair = hw // LANES
    q_spec = pl.BlockSpec((qb, hw), lambda h, b, i: (b * nq + i, h))
    kv = lambda off: pl.BlockSpec((qb, hw), lambda h, b, i: (b * nq + jnp.maximum(i - off, 0), h))
    return pl.pallas_call(
        functools.partial(_attn_prompt_kernel, qb=qb, npair=npair),
        grid=(d // hw, n_seq, nq),
        in_specs=[q_spec, kv(2), kv(1), kv(0), kv(2), kv(1), kv(0),
                  pl.BlockSpec((2 * npair, qb, 3 * qb), lambda h, b, i: (h, 0, 0))],
        out_specs=pl.BlockSpec((qb, hw), lambda h, b, i: (b * nq + i, h)),
        out_shape=jax.ShapeDtypeStruct((n_seq * seq_len, d), BF16),
        compiler_params=_params("parallel", "parallel", "arbitrary"),
        name="attn_prompt",
    )(q, k, k, k, v, v, v, bias)


def _attn_sample_kernel(q_ref, kn_ref, vn_ref, kc_ref, vc_ref, bc_ref, bn_ref, o_ref, *, npair):
    first = _head_masks()
    lanes = lambda p: slice(p * LANES, (p + 1) * LANES)
    units = []
    for p in range(npair):
        q = q_ref[:, lanes(p)]
        kc, kn = kc_ref[0, :, lanes(p)].astype(BF16), kn_ref[:, lanes(p)].astype(BF16)
        for h in range(2):
            qh = jnp.where(first if h == 0 else ~first, q, 0.0).astype(BF16)
            units.append(dict(p=p, h=2 * p + h, sc=_dot_nt(qh, kc), sn=_dot_nt(qh, kn)))
    for u in units:
        sc = u["sc"] + bc_ref[u["h"]]
        sn = u["sn"] + bn_ref[u["h"]]
        mx = jnp.maximum(jnp.max(sc, axis=-1, keepdims=True), jnp.max(sn, axis=-1, keepdims=True))
        pc, pn = jnp.exp(sc - mx), jnp.exp(sn - mx)
        u["denom"] = jnp.sum(pc, axis=-1, keepdims=True) + jnp.sum(pn, axis=-1, keepdims=True)
        u["pc"], u["pn"] = pc.astype(BF16), pn.astype(BF16)
    for u in units:
        vc, vn = vc_ref[0, :, lanes(u["p"])].astype(BF16), vn_ref[:, lanes(u["p"])].astype(BF16)
        u["o"] = (_dot(u["pc"], vc) + _dot(u["pn"], vn)) / u["denom"]
    for p in range(npair):
        o_ref[:, lanes(p)] = jnp.where(first, units[2 * p]["o"], units[2 * p + 1]["o"]).astype(o_ref.dtype)


def _attn_sample(q, k, v, cache_k, cache_v, bias_c, bias_n, *, n_seq, seq_len, row0):
    d = q.shape[1]
    kb = cache_k.shape[1]
    blk0 = row0 // seq_len
    new = pl.BlockSpec((seq_len, d), lambda s: (blk0 + s, 0))
    cache = pl.BlockSpec((1, kb, d), lambda s: (s, 0, 0))
    full = lambda a: pl.BlockSpec(a.shape, lambda s: (0,) * a.ndim)
    return pl.pallas_call(
        functools.partial(_attn_sample_kernel, npair=d // LANES),
        grid=(n_seq,),
        in_specs=[new, new, new, cache, cache, full(bias_c), full(bias_n)],
        out_specs=pl.BlockSpec((seq_len, d), lambda s: (s, 0)),
        out_shape=jax.ShapeDtypeStruct((n_seq * seq_len, d), BF16),
        compiler_params=_params("parallel"),
        name="attn_sample",
    )(q, k, v, cache_k, cache_v, bias_c, bias_n)


def _to_heads_kernel(*refs):
    n = len(refs) // 2
    for x_ref, o_ref in zip(refs[:n], refs[n:]):
        o_ref[...] = x_ref[...].reshape(o_ref.shape)


def _to_heads(arrs, *, row0, seq_stride, n_seq, rows):
    d = arrs[0].shape[1]
    tb = _tile(math.gcd(math.gcd(row0, seq_stride), rows), LANES, SUBLANES)
    in_spec = pl.BlockSpec((tb, d), lambda s, i: ((row0 + s * seq_stride) // tb + i, 0))
    out_spec = pl.BlockSpec((tb, d // HEAD, HEAD), lambda s, i: (s * (rows // tb) + i, 0, 0))
    return pl.pallas_call(
        _to_heads_kernel,
        grid=(n_seq, rows // tb),
        in_specs=[in_spec] * len(arrs),
        out_specs=[out_spec] * len(arrs),
        out_shape=[jax.ShapeDtypeStruct((n_seq * rows, d // HEAD, HEAD), a.dtype) for a in arrs],
        compiler_params=_params("parallel", "parallel"),
        name="to_heads",
    )(*arrs)


def _prompt_bias(table, qb):
    width = 3 * qb
    period = width + qb
    dist = (2 * qb - jnp.arange(period)) % period
    dist = jnp.where(dist > width, dist - period, dist)
    by_dist = table[:, jnp.clip(dist, -REL_CLIP, REL_CLIP) + REL_CLIP].astype(F32)
    heads = table.shape[0]
    rolled = jnp.tile(by_dist, (1, qb))[:, :qb * (period - 1)].reshape(heads, qb, period - 1)
    bias = rolled[:, :, :width]
    qi = jnp.arange(qb)[:, None]
    kj = jnp.arange(width)[None, :] - 2 * qb
    qc, kc = qi // CHUNK, jnp.floor_divide(kj, CHUNK)
    band = (kc <= qc) & (kc >= qc - PREV_CHUNKS)
    return jnp.where(band[None], bias, NEG_INF)


def _sample_bias(table, kb, t):
    dist = (kb + jnp.arange(t))[:, None] - jnp.arange(kb + t)[None, :]
    bias = table[:, jnp.clip(dist, -REL_CLIP, REL_CLIP) + REL_CLIP].astype(F32)
    return bias[:, :, :kb], bias[:, :, kb:]


def _swiglu_step(xb, w1, w3, w2):
    h1 = _dot(xb, w1)
    h3 = _dot(xb, w3)
    return _dot((h1 * jax.nn.sigmoid(h1) * h3).astype(BF16), w2)


def _dense_ffn_kernel(x_ref, w1_ref, w3_ref, w2_ref, g_ref, b_ref, o_ref, xb_ref, *, nf, alpha):
    f = pl.program_id(1)

    @pl.when(f == 0)
    def _():
        o_ref[...] = jnp.zeros(o_ref.shape, F32)
        xb_ref[...] = x_ref[...].astype(BF16)

    o_ref[...] += _swiglu_step(xb_ref[...], w1_ref[...].astype(BF16), w3_ref[...].astype(BF16),
                               w2_ref[...].astype(BF16))

    @pl.when(f == nf - 1)
    def _():
        o_ref[...] = _layer_norm(alpha * x_ref[...] + o_ref[...], g_ref[...], b_ref[...])


def _dense_ffn(x, w1, w3, w2, g, b, *, alpha, tm, tf=512):
    rows, d = x.shape
    dff = w1.shape[1]
    tf = _tile(dff, tf, LANES)
    nf = dff // tf
    row = pl.BlockSpec((tm, d), lambda t, f: (t, 0))
    vec = pl.BlockSpec((1, d), lambda t, f: (0, 0))
    w13 = pl.BlockSpec((d, tf), lambda t, f: (0, f))
    return pl.pallas_call(
        functools.partial(_dense_ffn_kernel, nf=nf, alpha=alpha),
        grid=(rows // tm, nf),
        in_specs=[row, w13, w13, pl.BlockSpec((tf, d), lambda t, f: (f, 0)), vec, vec],
        out_specs=row,
        out_shape=jax.ShapeDtypeStruct((rows, d), F32),
        scratch_shapes=[pltpu.VMEM((tm, d), BF16)],
        compiler_params=_params("parallel", "arbitrary"),
        name="dense_ffn",
    )(x, w1.astype(BF16), w3.astype(BF16), w2.astype(BF16), g, b)


def _router_kernel(x_ref, w_ref, o_ref, *, n_exp):
    x_hi, x_lo = _split_bf16(x_ref[...])
    w_hi, w_lo = _split_bf16(w_ref[...])
    logits = _dot(x_hi, w_hi) + _dot(x_hi, w_lo) + _dot(x_lo, w_hi)
    lane = lax.broadcasted_iota(jnp.int32, logits.shape, 1).astype(F32)
    valid = lane < n_exp
    lg = jnp.where(valid, logits, NEG_INF)
    ex = jnp.where(valid, jnp.exp(lg - jnp.max(lg, axis=-1, keepdims=True)), 0.0)
    probs = jnp.where(valid, ex / jnp.sum(ex, axis=-1, keepdims=True), -1.0)
    p1 = jnp.max(probs, axis=-1, keepdims=True)
    i1 = jnp.min(jnp.where(probs == p1, lane, float(LANES)), axis=-1, keepdims=True)
    rest = jnp.where(lane == i1, -1.0, probs)
    p2 = jnp.max(rest, axis=-1, keepdims=True)
    i2 = jnp.min(jnp.where(rest == p2, lane, float(LANES)), axis=-1, keepdims=True)
    tot = p1 + p2
    out = jnp.where(lane == 0, i1,
                    jnp.where(lane == 1, i2,
                              jnp.where(lane == 2, p1 / tot, jnp.where(lane == 3, p2 / tot, 0.0))))
    o_ref[...] = out


def _router(x, w, *, tm=512):
    m, d = x.shape
    n_exp = w.shape[1]
    wp = jnp.zeros((d, LANES), F32).at[:, :n_exp].set(w)
    tm = _tile(m, tm)
    return pl.pallas_call(
        functools.partial(_router_kernel, n_exp=n_exp),
        grid=(m // tm,),
        in_specs=[pl.BlockSpec((tm, d), lambda i: (i, 0)), pl.BlockSpec((d, LANES), lambda i: (0, 0))],
        out_specs=pl.BlockSpec((tm, LANES), lambda i: (i, 0)),
        out_shape=jax.ShapeDtypeStruct((m, LANES), F32),
        compiler_params=_params("parallel"),
        name="router",
    )(x, wp)


DMA_UNROLL = 8


def _row_copy(src_ref, dst_ref, sem, src_row, dst_row):
    return pltpu.make_async_copy(src_ref.at[pl.ds(src_row, 1)], dst_ref.at[pl.ds(dst_row, 1)], sem)


def _moe_ffn_kernel(te_ref, rows_ref, na_ref, src_ref, x_hbm, w1_ref, w3_ref, w2_ref, o_ref,
                    gbuf, xb_ref, sem, *, n_tiles, tm, sub, n_issue):
    t, f = pl.program_id(0), pl.program_id(1)
    rows = rows_ref[t]
    chunk = tm // n_issue

    def issue(tile, lo, n):
        def body(i, c):
            _row_copy(x_hbm, gbuf, sem, src_ref[tile * tm + lo + i], lo + i).start()
            return c
        lax.fori_loop(0, n, body, 0, unroll=DMA_UNROLL)

    def wait_tile():
        def body(i, c):
            _row_copy(x_hbm, gbuf, sem, 0, i).wait()
            return c
        lax.fori_loop(0, tm, body, 0, unroll=DMA_UNROLL)

    @pl.when(jnp.logical_and(jnp.logical_and(t == 0, f == 0), rows > 0))
    def _():
        issue(0, 0, tm)

    @pl.when(f == 0)
    def _():
        o_ref[...] = jnp.zeros(o_ref.shape, F32)

        @pl.when(rows > 0)
        def _():
            wait_tile()
            xb_ref[...] = gbuf[...].astype(BF16)

    nxt = jnp.minimum(t + 1, n_tiles - 1)
    fetch_next = jnp.logical_and(jnp.logical_and(f >= 1, f <= n_issue),
                                 jnp.logical_and(t + 1 < n_tiles, rows_ref[nxt] > 0))

    @pl.when(fetch_next)
    def _():
        issue(t + 1, (f - 1) * chunk, chunk)

    for n in range(sub, tm + 1, sub):
        @pl.when(jnp.logical_and(rows > n - sub, rows <= n))
        def _():
            o_ref[:n, :] += _swiglu_step(xb_ref[:n, :], w1_ref[...].astype(BF16),
                                         w3_ref[...].astype(BF16), w2_ref[...].astype(BF16))


def _moe_ffn(x, src, w1, w3, w2, layer, tile_expert, tile_rows, n_active, *, tm, tf):
    d = x.shape[1]
    n_slots = src.shape[0]
    n_tiles = n_slots // tm
    dff = w1.shape[3]
    tf = _tile(dff, min(tf, dff // 2), LANES)
    nf = dff // tf
    n_issue = 1
    while 2 * n_issue <= max(nf - 1, 1) and tm % (2 * n_issue) == 0:
        n_issue *= 2
    assert nf >= 2
    sub = next(s for s in (tm // 3, tm // 4, tm // 2, tm) if s * (tm // s) == tm and s % 16 == 0)

    def clamp(t, na):
        return jnp.minimum(t, na[0] - 1)

    def f_idx(t, f, na):
        return jnp.where(t < na[0], f, nf - 1)

    w13 = pl.BlockSpec((None, None, d, tf),
                       lambda t, f, te, tr, na, sr: (layer, te[clamp(t, na)], 0, f_idx(t, f, na)))
    w2s = pl.BlockSpec((None, None, tf, d),
                       lambda t, f, te, tr, na, sr: (layer, te[clamp(t, na)], f_idx(t, f, na), 0))
    return pl.pallas_call(
        functools.partial(_moe_ffn_kernel, n_tiles=n_tiles, tm=tm, sub=sub, n_issue=n_issue),
        grid_spec=pltpu.PrefetchScalarGridSpec(
            num_scalar_prefetch=4,
            grid=(n_tiles, nf),
            in_specs=[pl.BlockSpec(memory_space=pl.ANY), w13, w13, w2s],
            out_specs=pl.BlockSpec((tm, d), lambda t, f, te, tr, na, sr: (t, 0),
                                   pipeline_mode=pl.Buffered(1)),
            scratch_shapes=[pltpu.VMEM((tm, d), F32), pltpu.VMEM((tm, d), BF16),
                            pltpu.SemaphoreType.DMA(())]),
        out_shape=jax.ShapeDtypeStruct((n_slots, d), F32),
        compiler_params=_params("arbitrary", "arbitrary"),
        name="moe_ffn",
    )(tile_expert, tile_rows, n_active, src, x, w1, w3, w2)


def _combine_kernel(i0_ref, i1_ref, y_ref, x_ref, sel_ref, g_ref, b_ref, o_ref, buf, sem, *, tc, alpha):
    base = pl.program_id(0) * tc

    def start(i, c):
        _row_copy(y_ref, buf.at[0], sem, i0_ref[base + i], i).start()
        _row_copy(y_ref, buf.at[1], sem, i1_ref[base + i], i).start()
        return c

    def wait(i, c):
        _row_copy(y_ref, buf.at[0], sem, 0, i).wait()
        _row_copy(y_ref, buf.at[1], sem, 0, i).wait()
        return c

    lax.fori_loop(0, tc, start, 0, unroll=DMA_UNROLL)
    lax.fori_loop(0, tc, wait, 0, unroll=DMA_UNROLL)
    sel = sel_ref[...]
    mixed = buf[0] * sel[:, TOP_K:TOP_K + 1] + buf[1] * sel[:, TOP_K + 1:TOP_K + 2]
    o_ref[...] = _layer_norm(alpha * x_ref[...] + mixed, g_ref[...], b_ref[...])


def _combine(y, x, sel, i0, i1, g, b, *, alpha, tc=256):
    m, d = x.shape
    tc = _tile(m, tc)
    row = pl.BlockSpec((tc, d), lambda i, a, c: (i, 0))
    vec = pl.BlockSpec((1, d), lambda i, a, c: (0, 0))
    return pl.pallas_call(
        functools.partial(_combine_kernel, tc=tc, alpha=alpha),
        grid_spec=pltpu.PrefetchScalarGridSpec(
            num_scalar_prefetch=2,
            grid=(m // tc,),
            in_specs=[pl.BlockSpec(memory_space=pl.ANY), row,
                      pl.BlockSpec((tc, LANES), lambda i, a, c: (i, 0)), vec, vec],
            out_specs=row,
            scratch_shapes=[pltpu.VMEM((2, tc, d), F32), pltpu.SemaphoreType.DMA(())]),
        out_shape=jax.ShapeDtypeStruct((m, d), F32),
        compiler_params=_params("arbitrary"),
        name="moe_combine",
    )(i0, i1, y, x, sel, g, b)


def _moe(x, router, w1, w3, w2, layer, g, b, *, alpha, tm):
    m, d = x.shape
    n_exp = w1.shape[1]
    sel = _router(x, router)
    top_i = sel[:, :TOP_K].astype(jnp.int32)
    e_flat = top_i.reshape(-1)
    onehot = (e_flat[:, None] == jnp.arange(n_exp)[None, :]).astype(jnp.int32)
    rank = jnp.sum((jnp.cumsum(onehot, axis=0) - onehot) * onehot, axis=1)
    counts = jnp.sum(onehot, axis=0)
    tiles_e = (counts + tm - 1) // tm
    per_tile = (counts + jnp.maximum(tiles_e, 1) - 1) // jnp.maximum(tiles_e, 1)
    per_tile = jnp.maximum((per_tile + SUBLANES - 1) // SUBLANES * SUBLANES, SUBLANES)
    tile_end = jnp.cumsum(tiles_e)
    tile_base = tile_end - tiles_e
    pick = lambda v: jnp.sum(onehot * v[None, :], axis=1)
    k_in_e = rank // pick(per_tile)
    slot = (pick(tile_base) + k_in_e) * tm + rank - k_in_e * pick(per_tile)
    n_tiles = (TOP_K * m + n_exp * (tm - 1)) // tm
    n_slots = n_tiles * tm
    src = jnp.zeros((n_slots,), jnp.int32).at[slot].set(jnp.arange(TOP_K * m, dtype=jnp.int32) // TOP_K)
    tile_id = jnp.arange(n_tiles, dtype=jnp.int32)
    tile_expert = jnp.minimum(jnp.sum((tile_end[None, :] <= tile_id[:, None]).astype(jnp.int32), axis=1),
                              n_exp - 1).astype(jnp.int32)
    k_of_tile = tile_id - tile_base[tile_expert]
    tile_rows = jnp.clip(counts[tile_expert] - k_of_tile * per_tile[tile_expert], 0, per_tile[tile_expert])
    tile_rows = jnp.where(tile_id < tile_end[-1], tile_rows, 0).astype(jnp.int32)
    n_active = tile_end[-1:].astype(jnp.int32)

    ys = _moe_ffn(x, src, w1, w3, w2, layer, tile_expert, tile_rows, n_active, tm=tm, tf=512)
    slot2 = slot.reshape(m, TOP_K).astype(jnp.int32)
    return _combine(ys, x, sel, slot2[:, 0], slot2[:, 1], g, b, alpha=alpha)


def kernel(x_prompt, x_sample, state_wkv, state_shift, cache_k, cache_v, ln_g, ln_b, rwkv_mu, rwkv_wr, rwkv_wk, rwkv_wv, rwkv_wo, rwkv_w0, rwkv_w1, rwkv_w2, rwkv_a0, rwkv_a1, rwkv_a2, rwkv_v0, rwkv_v1, rwkv_v2, rwkv_g1, rwkv_g2, rwkv_k_k, rwkv_k_a, rwkv_r_k, rwkv_lnx_g, rwkv_lnx_b, attn_wk, attn_wv, attn_wq, attn_wo, attn_rel_bias, ffn_w1, ffn_w3, ffn_w2, moe_router, moe_w1, moe_w3, moe_w2):
    bp, tp, d = x_prompt.shape
    bs, ts, _ = x_sample.shape
    depth = ln_g.shape[0]
    n_a = rwkv_wr.shape[0]
    alpha = (2 * depth) ** 0.25
    mp, ms = bp * tp, bs * ts
    m = mp + ms
    kb = cache_k.shape[1]
    assert tp % CHUNK == 0 and d % LANES == 0 and mp % ts == 0
    row = lambda a: a.reshape(1, -1)
    bf = lambda a: a.astype(BF16)
    qb = _tile(tp, 256, CHUNK)
    assert 2 * qb >= PREV
    tm_dense = _tile(m, 640)
    n_exp = moe_w1.shape[1]
    tm_moe = min(MOE_TILE_MAX, -(-int(TOP_K * m / n_exp / 2 * 1.05) // LANES) * LANES)
    tm_pre = _tile(math.gcd(mp, ms), 256)

    x = jnp.concatenate([x_prompt.reshape(mp, d), x_sample.reshape(ms, d)], axis=0)
    ridx = jnp.arange(m)
    start_flag = jnp.where(ridx < mp, (ridx % tp == 0) * 1.0, ((ridx - mp) % ts == 0) * 2.0)
    start_flag = start_flag.astype(F32).reshape(m, 1)

    p_wkv, s_wkv, p_shift, s_shift = [], [], [], []
    v_first = k_sh = v_sh = None
    for l in range(depth):
        g0, b0, g1, b1 = row(ln_g[l, 0]), row(ln_b[l, 0]), row(ln_g[l, 1]), row(ln_b[l, 1])
        if l < n_a:
            p_shift.append(x[tp - 1:mp:tp])
            s_shift.append(x[mp + ts - 1::ts])
            start_rows = jnp.repeat(state_shift[l], ts, axis=0)
            has_v = l > 0
            lora1 = [bf(rwkv_w1[l]), bf(rwkv_a1[l]), bf(rwkv_g1[l])] + ([bf(rwkv_v1[l - 1])] if has_v else [])
            lora2 = [bf(rwkv_w2[l]), bf(rwkv_a2[l]), bf(rwkv_g2[l])] + ([bf(rwkv_v2[l - 1])] if has_v else [])
            bias = [row(rwkv_w0[l]), row(rwkv_a0[l])] + ([row(rwkv_v0[l - 1])] if has_v else [])
            pre = _rwkv_pre(x, start_rows, start_flag, rwkv_mu[l], lora1, lora2, bias, tm=tm_pre)
            xr, xk, xv, lw, ag, gg = pre[:6]
            r = _matmul(xr, rwkv_wr, layer=l, name="rwkv_r")
            k = _matmul(xk, rwkv_wk, layer=l, name="rwkv_k")
            if has_v:
                v = _matmul(xv, rwkv_wv, layer=l, name="rwkv_v", extras=(v_first, pre[6]),
                            epilogue=lambda acc, vf, vg: acc + (vf - acc) * vg)
            else:
                v = _matmul(xv, rwkv_wv, layer=l, name="rwkv_v")
                v_first = v
            seqs = (r, k, v, lw, ag, gg)
            vecs = [row(rwkv_k_k[l]), row(rwkv_k_a[l]), row(rwkv_r_k[l]), row(rwkv_lnx_g[l]), row(rwkv_lnx_b[l])]
            zp, stp = _wkv(seqs, vecs, None, name="wkv_prompt", n_seq=bp, seq_len=tp, row0=0, L=CHUNK, hw=2048)
            zs, sts = _wkv(seqs, vecs, state_wkv[l].astype(F32), name="wkv_sample",
                           n_seq=bs, seq_len=ts, row0=mp, L=ts, hw=2048)
            p_wkv.append(stp.astype(state_wkv.dtype))
            s_wkv.append(sts.astype(state_wkv.dtype))
            z = jnp.concatenate([zp, zs], axis=0)
            x = _matmul(z, bf(rwkv_wo[l]), name="rwkv_out_ln", extras=(x,), vecs=(g0, b0),
                        epilogue=_ln_epilogue(alpha), tm=256, tn=d)
        else:
            i = l - n_a
            q = _matmul(x, attn_wq, layer=i, name="attn_q", out_dtype=BF16,
                        epilogue=lambda acc: acc * (HEAD ** -0.5))
            table = attn_rel_bias[i]
            op = _attn_prompt(q, k_sh, v_sh, _prompt_bias(table, qb), n_seq=bp, seq_len=tp, qb=qb)
            bias_c, bias_n = _sample_bias(table, kb, ts)
            os_ = _attn_sample(q, k_sh, v_sh, cache_k.reshape(bs, kb, d), cache_v.reshape(bs, kb, d),
                               bias_c, bias_n, n_seq=bs, seq_len=ts, row0=mp)
            o = jnp.concatenate([op, os_], axis=0)
            x = _matmul(o, bf(attn_wo[i]), name="attn_out_ln", extras=(x,), vecs=(g0, b0),
                        epilogue=_ln_epilogue(alpha), tm=256, tn=d)
        j = l // 2
        if l % 2 == 0:
            x = _dense_ffn(x, ffn_w1[j], ffn_w3[j], ffn_w2[j], g1, b1, alpha=alpha, tm=tm_dense)
        else:
            x = _moe(x, moe_router[j], moe_w1, moe_w3, moe_w2, j, g1, b1, alpha=alpha, tm=tm_moe)
        if l == n_a - 1:
            k_sh = _matmul(x, attn_wk, name="attn_k")
            v_sh = _matmul(x, attn_wv, name="attn_v")

    heads = d // HEAD
    keep = min(PREV, tp)
    kp, vp = _to_heads((k_sh, v_sh), row0=tp - keep, seq_stride=tp, n_seq=bp, rows=keep)
    ks, vs = _to_heads((k_sh, v_sh), row0=mp, seq_stride=ms, n_seq=1, rows=ms)
    return (x[:mp].reshape(bp, tp, d), x[mp:].reshape(bs, ts, d),
            jnp.stack(p_wkv), jnp.stack(p_shift),
            kp.reshape(bp, keep, heads, HEAD), vp.reshape(bp, keep, heads, HEAD),
            jnp.stack(s_wkv), jnp.stack(s_shift),
            ks.reshape(bs, ts, heads, HEAD), vs.reshape(bs, ts, heads, HEAD))
```

```python
import functools
import math

import jax
import jax.numpy as jnp
from jax import lax
from jax.experimental import pallas as pl
from jax.experimental.pallas import tpu as pltpu

F32 = jnp.float32
BF16 = jnp.bfloat16

CHUNK = 64
PREV_CHUNKS = 8
PREV = PREV_CHUNKS * CHUNK
REL_CLIP = 128
HEAD = 64
TOP_K = 2
GN_EPS = 64e-5
LN_EPS = 1e-5
NEG_INF = -1e30

LANES = 128
SUBLANES = 8
VMEM_LIMIT_BYTES = 60 * 1024 * 1024
MOE_TILE_MAX = 1152


def _params(*sem):
    return pltpu.CompilerParams(dimension_semantics=sem, vmem_limit_bytes=VMEM_LIMIT_BYTES)


def _tile(n, target, mult=16):
    best = None
    for t in range(mult, min(n, target) + 1, mult):
        if n % t == 0:
            best = t
    assert best is not None, (n, target, mult)
    return best


def _dot(a, b):
    return jnp.dot(a, b, preferred_element_type=F32)


def _dot_nt(a, b):
    return lax.dot_general(a, b, (((1,), (1,)), ((), ())), preferred_element_type=F32)


def _dot_tn(a, b):
    return lax.dot_general(a, b, (((0,), (0,)), ((), ())), preferred_element_type=F32)


def _split_bf16(x):
    hi = x.astype(BF16)
    lo = (x - hi.astype(F32)).astype(BF16)
    return hi, lo


def _layer_norm(y, g, b):
    mu = jnp.mean(y, axis=-1, keepdims=True)
    dev = y - mu
    var = jnp.mean(dev * dev, axis=-1, keepdims=True)
    return dev * lax.rsqrt(var + LN_EPS) * g + b


def _softplus(u):
    return jnp.maximum(u, 0.0) + jnp.log(1.0 + jnp.exp(-jnp.abs(u)))


def _mm_kernel(*refs, n_extra, n_vec, epilogue, cast_w):
    x_ref, w_ref = refs[0], refs[1]
    extra = refs[2:2 + n_extra]
    vecs = refs[2 + n_extra:2 + n_extra + n_vec]
    o_ref = refs[2 + n_extra + n_vec]
    if cast_w:
        wb_ref = refs[-1]

        @pl.when(pl.program_id(1) == 0)
        def _():
            wb_ref[...] = w_ref[...].astype(BF16)

        w = wb_ref[...]
    else:
        w = w_ref[...]
    acc = _dot(x_ref[...].astype(BF16), w)
    if epilogue is not None:
        acc = epilogue(acc, *[e[...] for e in extra], *[v[...] for v in vecs])
    o_ref[...] = acc.astype(o_ref.dtype)


def _matmul(x, w, *, name, layer=None, out_dtype=F32, epilogue=None, extras=(), vecs=(), tm=512, tn=1024):
    m, k = x.shape
    n = w.shape[-1]
    tm = _tile(m, tm)
    tn = n if n <= tn else _tile(n, tn, LANES)
    cast_w = w.dtype != BF16
    if w.ndim == 3:
        w_spec = pl.BlockSpec((None, k, tn), lambda j, i: (layer, 0, j))
    else:
        w_spec = pl.BlockSpec((k, tn), lambda j, i: (0, j))
    in_specs = [pl.BlockSpec((tm, k), lambda j, i: (i, 0)), w_spec]
    in_specs += [pl.BlockSpec((tm, tn), lambda j, i: (i, j)) for _ in extras]
    in_specs += [pl.BlockSpec((1, tn), lambda j, i: (0, j)) for _ in vecs]
    return pl.pallas_call(
        functools.partial(_mm_kernel, n_extra=len(extras), n_vec=len(vecs),
                          epilogue=epilogue, cast_w=cast_w),
        grid=(n // tn, m // tm),
        in_specs=in_specs,
        out_specs=pl.BlockSpec((tm, tn), lambda j, i: (i, j)),
        out_shape=jax.ShapeDtypeStruct((m, n), out_dtype),
        scratch_shapes=[pltpu.VMEM((k, tn), BF16)] if cast_w else [],
        compiler_params=_params("parallel", "arbitrary"),
        name=name,
    )(x, w, *extras, *vecs)


def _ln_epilogue(alpha):
    def epi(acc, xres, g, b):
        return _layer_norm(alpha * xres + acc, g, b)
    return epi


def _rwkv_pre_kernel(*refs, has_v):
    it = iter(refs)
    x_ref, halo_ref, start_ref, flag_ref, mu_ref = (next(it) for _ in range(5))
    w1, a1, g1 = next(it), next(it), next(it)
    v1 = next(it) if has_v else None
    w2, a2, g2 = next(it), next(it), next(it)
    v2 = next(it) if has_v else None
    w0, a0 = next(it), next(it)
    v0 = next(it) if has_v else None
    xr_ref, xk_ref, xv_ref, lw_ref, ag_ref, g_ref = (next(it) for _ in range(6))
    vg_ref = next(it) if has_v else None

    x = x_ref[...]
    row = lax.broadcasted_iota(jnp.int32, (x.shape[0], 1), 0)
    prev = jnp.where(row == 0, halo_ref[SUBLANES - 1:SUBLANES, :], pltpu.roll(x, 1, 0))
    flag = flag_ref[...]
    prev = jnp.where(flag == 1.0, 0.0, jnp.where(flag == 2.0, start_ref[...], prev))
    xx = prev - x
    mix = lambda i: (x + xx * mu_ref[i:i + 1, :]).astype(BF16)
    xr_ref[...] = mix(0)
    xk_ref[...] = mix(2)
    xv = mix(3)
    xv_ref[...] = xv
    hw = jnp.tanh(_dot(mix(1), w1[...]))
    z = w0[...] + _dot(hw.astype(BF16), w2[...])
    lw_ref[...] = -jnp.exp(-_softplus(-z) - 0.5)
    ha = _dot(mix(4), a1[...])
    ag_ref[...] = jax.nn.sigmoid(a0[...] + _dot(ha.astype(BF16), a2[...]))
    hg = jax.nn.sigmoid(_dot(mix(5), g1[...]))
    g_ref[...] = _dot(hg.astype(BF16), g2[...])
    if has_v:
        hv = _dot(xv, v1[...])
        vg_ref[...] = jax.nn.sigmoid(v0[...] + _dot(hv.astype(BF16), v2[...]))


def _rwkv_pre(x, start_rows, flag, mu, lora1, lora2, bias, *, tm):
    m, d = x.shape
    first_start_tile = (m - start_rows.shape[0]) // tm
    has_v = len(lora1) == 4
    row = pl.BlockSpec((tm, d), lambda i: (i, 0))
    halo = pl.BlockSpec((SUBLANES, d), lambda i: (jnp.maximum(i * (tm // SUBLANES) - 1, 0), 0))
    start = pl.BlockSpec((tm, d), lambda i: (jnp.maximum(i - first_start_tile, 0), 0))
    full = lambda a: pl.BlockSpec(a.shape, lambda i: (0,) * a.ndim)
    ins = [x, x, start_rows, flag, mu, *lora1, *lora2, *bias]
    in_specs = [row, halo, start, pl.BlockSpec((tm, 1), lambda i: (i, 0))] + [full(a) for a in ins[4:]]
    out_shape = [jax.ShapeDtypeStruct((m, d), BF16)] * 3 + [jax.ShapeDtypeStruct((m, d), F32)] * (3 + has_v)
    return pl.pallas_call(
        functools.partial(_rwkv_pre_kernel, has_v=has_v),
        grid=(m // tm,),
        in_specs=in_specs,
        out_specs=[row] * len(out_shape),
        out_shape=out_shape,
        compiler_params=_params("parallel"),
        name="rwkv_pre",
    )(*ins)


def _wkv_kernel(*refs, L, npair, nc, has_init):
    it = iter(refs)
    r_ref, k_ref, v_ref, lw_ref, ag_ref, g_ref = (next(it) for _ in range(6))
    kk_ref, ka_ref, rk_ref, lg_ref, lb_ref = (next(it) for _ in range(5))
    s0_ref = next(it) if has_init else None
    z_ref, so_ref, st_ref = next(it), next(it), next(it)
    L2 = 2 * L
    pairs = range(npair)
    bf = lambda t: t.astype(BF16)

    @pl.when(pl.program_id(2) == 0)
    def _():
        if has_init:
            zero = jnp.zeros((HEAD, HEAD), F32)
            for p in pairs:
                top = jnp.concatenate([s0_ref[0, 2 * p], zero], axis=1)
                bot = jnp.concatenate([zero, s0_ref[0, 2 * p + 1]], axis=1)
                st_ref[p] = jnp.concatenate([top, bot], axis=0).T
        else:
            st_ref[...] = jnp.zeros(st_ref.shape, F32)

    def iota(shape, dim):
        return lax.broadcasted_iota(jnp.int32, shape, dim)

    tri = (iota((L, L), 0) >= iota((L, L), 1)).astype(BF16)
    lw = lw_ref[...]
    lw_hi, lw_lo = _split_bf16(lw)
    cum = _dot(tri, lw_hi) + _dot(tri, lw_lo)
    cum_last = cum[L - 1:L, :]
    e_pos = jnp.exp(cum)
    e_neg = jnp.exp(-cum)
    e_prev = jnp.exp(cum - lw)
    e_last = jnp.exp(cum_last - cum)
    lw_split = jnp.concatenate([lw_hi, lw_lo], axis=0)

    rows, cols = iota((L2, L2), 0), iota((L2, L2), 1)
    same_head = (rows >= L) == (cols >= L)
    lower_strict = same_head & (rows > cols)
    lower_incl = same_head & (rows >= cols)
    eye = (rows == cols).astype(F32)
    own_lanes = ((iota((L2, LANES), 0) >= L) == (iota((L2, LANES), 1) >= HEAD)).astype(F32)
    ones = jnp.ones((L2, LANES), BF16)
    dup = lambda t: jnp.concatenate([t, t], axis=0)
    stack = lambda t: dup(t) * own_lanes
    unstack = lambda t: t[:L] + t[L:]
    lanes = lambda p: slice(p * LANES, (p + 1) * LANES)
    left, right = slice(0, LANES), slice(LANES, 2 * LANES)

    ops = []
    for p in pairs:
        sl = lanes(p)
        ag = ag_ref[:, sl]
        kraw = k_ref[:, sl]
        kk = stack(kraw * kk_ref[:, sl])
        kk = kk / jnp.maximum(jnp.sqrt(jnp.sum(kk * kk, axis=-1, keepdims=True)), 1e-12)
        ks = stack(kraw * (1.0 + (ag - 1.0) * ka_ref[:, sl]))
        rs = stack(r_ref[:, sl])
        vs = stack(v_ref[:, sl])
        b = kk * dup(ag)
        en = dup(e_neg[:, sl])
        el = dup(e_last[:, sl])
        ops.append(dict(
            ks=ks, rs=rs, vs=vs, vb=bf(vs),
            at=bf(-kk * dup(e_prev[:, sl])), rt=rs * dup(e_pos[:, sl]),
            bt=bf(b * en), kt=bf(ks * en), bh=bf(b * el), kh=bf(ks * el)))

    for o in ops:
        o["rtb"] = bf(o["rt"])
        if L2 % LANES == 0:
            m = _dot_nt(jnp.concatenate([o["at"], o["rtb"]], axis=0),
                        jnp.concatenate([o["bt"], o["kt"]], axis=0))
            m_ab, m_ak, m_rb, m_rk = m[:L2, :L2], m[:L2, L2:], m[L2:, :L2], m[L2:, L2:]
        else:
            m_ab, m_ak = _dot_nt(o["at"], o["bt"]), _dot_nt(o["at"], o["kt"])
            m_rb, m_rk = _dot_nt(o["rtb"], o["bt"]), _dot_nt(o["rtb"], o["kt"])
        o["m_ab"] = jnp.where(lower_strict, m_ab, 0.0)
        o["m_ak"] = bf(jnp.where(lower_strict, m_ak, 0.0))
        o["m_rb"] = bf(jnp.where(lower_incl, m_rb, 0.0))
        o["m_rk"] = bf(jnp.where(lower_incl, m_rk, 0.0))

    for o in ops:
        o["tinv"] = eye + o["m_ab"]
        o["mp"] = o["m_ab"]
    span = 2
    while span < L:
        for o in ops:
            mpb = bf(o["mp"])
            o["mp"] = _dot(mpb, mpb)
        for o in ops:
            o["tinv"] = o["tinv"] + _dot(bf(o["tinv"]), bf(o["mp"]))
        span *= 2

    for o in ops:
        o["makv"] = bf(_dot(o["m_ak"], o["vb"]))
    for o in ops:
        au = _dot(bf(o["tinv"]), jnp.concatenate([o["at"], o["makv"]], axis=1))
        o["aub"] = bf(au)
    for o in ops:
        ry = _dot(o["m_rb"], o["aub"])
        o["rh"] = bf(o["rt"] + ry[:, left])
        o["y0"] = ry[:, right] + _dot(o["m_rk"], o["vb"])
        gq = _dot_tn(o["bh"], o["aub"])
        o["g"] = bf(gq[:, left])
        o["q"] = gq[:, right] + _dot_tn(o["kh"], o["vb"])
    last = pl.program_id(2) == nc - 1
    for p, o in zip(pairs, ops):
        decay = jnp.exp(_dot_tn(lw_split[:, lanes(p)], ones))
        st = st_ref[p]
        stb = bf(st)
        o["ys"] = _dot(o["rh"], stb) + o["y0"]
        st_ref[p] = decay * st + _dot(o["g"], stb) + o["q"]

    @pl.when(last)
    def _():
        for p in pairs:
            t = st_ref[p].T
            so_ref[0, 2 * p] = t[:HEAD, :HEAD]
            so_ref[0, 2 * p + 1] = t[HEAD:, HEAD:]

    for p, o in zip(pairs, ops):
        sl = lanes(p)
        ys = o["ys"]
        mean = jnp.sum(ys, axis=-1, keepdims=True) * (1.0 / HEAD)
        dev = (ys - mean) * own_lanes
        var = jnp.sum(dev * dev, axis=-1, keepdims=True) * (1.0 / HEAD)
        yn = unstack(dev * lax.rsqrt(var + GN_EPS)) * lg_ref[:, sl] + lb_ref[:, sl]
        bonus = unstack(jnp.sum(o["rs"] * o["ks"] * rk_ref[:, sl], axis=-1, keepdims=True) * o["vs"])
        z_ref[:, sl] = ((yn + bonus) * g_ref[:, sl]).astype(z_ref.dtype)


def _wkv(seqs, vecs, s0, *, name, n_seq, seq_len, row0, L, hw):
    d = seqs[0].shape[1]
    hw = min(hw, d)
    nc = seq_len // L
    blk0 = row0 // L
    npair = hw // LANES
    seq_spec = pl.BlockSpec((L, hw), lambda b, h, c: (blk0 + b * nc + c, h))
    vec_spec = pl.BlockSpec((1, hw), lambda b, h, c: (0, h))
    st_spec = pl.BlockSpec((1, 2 * npair, HEAD, HEAD), lambda b, h, c: (b, h, 0, 0))
    in_specs = [seq_spec] * 6 + [vec_spec] * 5 + ([st_spec] if s0 is not None else [])
    ins = list(seqs) + list(vecs) + ([s0] if s0 is not None else [])
    return pl.pallas_call(
        functools.partial(_wkv_kernel, L=L, npair=npair, nc=nc, has_init=s0 is not None),
        grid=(n_seq, d // hw, nc),
        in_specs=in_specs,
        out_specs=[pl.BlockSpec((L, hw), lambda b, h, c: (b * nc + c, h)), st_spec],
        out_shape=[jax.ShapeDtypeStruct((n_seq * seq_len, d), BF16),
                   jax.ShapeDtypeStruct((n_seq, d // HEAD, HEAD, HEAD), F32)],
        scratch_shapes=[pltpu.VMEM((npair, LANES, LANES), F32)],
        compiler_params=_params("parallel", "parallel", "arbitrary"),
        name=name,
    )(*ins)


def _head_masks():
    lane = lax.broadcasted_iota(jnp.int32, (1, LANES), 1)
    return lane < HEAD


def _attn_prompt_kernel(q_ref, k0, k1, k2, v0, v1, v2, bias_ref, o_ref, *, qb, npair):
    first = _head_masks()
    blk = pl.program_id(2)
    lanes = lambda p: slice(p * LANES, (p + 1) * LANES)
    cat = lambda refs, p: jnp.concatenate([r[:, lanes(p)] for r in refs], axis=0).astype(BF16)
    kpos = lax.broadcasted_iota(jnp.int32, (1, 3 * qb), 1) + (blk - 2) * qb
    valid = kpos >= 0
    units = []
    for p in range(npair):
        q = q_ref[:, lanes(p)]
        kcat = cat((k0, k1, k2), p)
        for h in range(2):
            qh = jnp.where(first if h == 0 else ~first, q, 0.0).astype(BF16)
            units.append(dict(p=p, h=2 * p + h, s=_dot_nt(qh, kcat)))
    for u in units:
        s = jnp.where(valid, u["s"] + bias_ref[u["h"]], NEG_INF)
        e = jnp.exp(s - jnp.max(s, axis=-1, keepdims=True))
        u["denom"] = jnp.sum(e, axis=-1, keepdims=True)
        u["e"] = e.astype(BF16)
    for p in range(npair):
        vcat = cat((v0, v1, v2), p)
        outs = [_dot(u["e"], vcat) / u["denom"] for u in units[2 * p:2 * p + 2]]
        o_ref[:, lanes(p)] = jnp.where(first, outs[0], outs[1]).astype(o_ref.dtype)


def _attn_prompt(q, k, v, bias, *, n_seq, seq_len, qb, hw=4 * LANES):
    d = q.shape[1]
    nq = seq_len // qb
    hw = min(hw, d)
    npair = hw // LANES
    q_spec = pl.BlockSpec((qb, hw), lambda h, b, i: (b * nq + i, h))
    kv = lambda off: pl.BlockSpec((qb, hw), lambda h, b, i: (b * nq + jnp.maximum(i - off, 0), h))
    return pl.pallas_call(
        functools.partial(_attn_prompt_kernel, qb=qb, npair=npair),
        grid=(d // hw, n_seq, nq),
        in_specs=[q_spec, kv(2), kv(1), kv(0), kv(2), kv(1), kv(0),
                  pl.BlockSpec((2 * npair, qb, 3 * qb), lambda h, b, i: (h, 0, 0))],
        out_specs=pl.BlockSpec((qb, hw), lambda h, b, i: (b * nq + i, h)),
        out_shape=jax.ShapeDtypeStruct((n_seq * seq_len, d), BF16),
        compiler_params=_params("parallel", "parallel", "arbitrary"),
        name="attn_prompt",
    )(q, k, k, k, v, v, v, bias)


def _attn_sample_kernel(q_ref, kn_ref, vn_ref, kc_ref, vc_ref, bc_ref, bn_ref, o_ref, *, npair):
    first = _head_masks()
    lanes = lambda p: slice(p * LANES, (p + 1) * LANES)
    units = []
    for p in range(npair):
        q = q_ref[:, lanes(p)]
        kc, kn = kc_ref[0, :, lanes(p)].astype(BF16), kn_ref[:, lanes(p)].astype(BF16)
        for h in range(2):
            qh = jnp.where(first if h == 0 else ~first, q, 0.0).astype(BF16)
            units.append(dict(p=p, h=2 * p + h, sc=_dot_nt(qh, kc), sn=_dot_nt(qh, kn)))
    for u in units:
        sc = u["sc"] + bc_ref[u["h"]]
        sn = u["sn"] + bn_ref[u["h"]]
        mx = jnp.maximum(jnp.max(sc, axis=-1, keepdims=True), jnp.max(sn, axis=-1, keepdims=True))
        pc, pn = jnp.exp(sc - mx), jnp.exp(sn - mx)
        u["denom"] = jnp.sum(pc, axis=-1, keepdims=True) + jnp.sum(pn, axis=-1, keepdims=True)
        u["pc"], u["pn"] = pc.astype(BF16), pn.astype(BF16)
    for u in units:
        vc, vn = vc_ref[0, :, lanes(u["p"])].astype(BF16), vn_ref[:, lanes(u["p"])].astype(BF16)
        u["o"] = (_dot(u["pc"], vc) + _dot(u["pn"], vn)) / u["denom"]
    for p in range(npair):
        o_ref[:, lanes(p)] = jnp.where(first, units[2 * p]["o"], units[2 * p + 1]["o"]).astype(o_ref.dtype)


def _attn_sample(q, k, v, cache_k, cache_v, bias_c, bias_n, *, n_seq, seq_len, row0):
    d = q.shape[1]
    kb = cache_k.shape[1]
    blk0 = row0 // seq_len
    new = pl.BlockSpec((seq_len, d), lambda s: (blk0 + s, 0))
    cache = pl.BlockSpec((1, kb, d), lambda s: (s, 0, 0))
    full = lambda a: pl.BlockSpec(a.shape, lambda s: (0,) * a.ndim)
    return pl.pallas_call(
        functools.partial(_attn_sample_kernel, npair=d // LANES),
        grid=(n_seq,),
        in_specs=[new, new, new, cache, cache, full(bias_c), full(bias_n)],
        out_specs=pl.BlockSpec((seq_len, d), lambda s: (s, 0)),
        out_shape=jax.ShapeDtypeStruct((n_seq * seq_len, d), BF16),
        compiler_params=_params("parallel"),
        name="attn_sample",
    )(q, k, v, cache_k, cache_v, bias_c, bias_n)


def _to_heads_kernel(*refs):
    n = len(refs) // 2
    for x_ref, o_ref in zip(refs[:n], refs[n:]):
        o_ref[...] = x_ref[...].reshape(o_ref.shape)


def _to_heads(arrs, *, row0, seq_stride, n_seq, rows):
    d = arrs[0].shape[1]
    tb = _tile(math.gcd(math.gcd(row0, seq_stride), rows), LANES, SUBLANES)
    in_spec = pl.BlockSpec((tb, d), lambda s, i: ((row0 + s * seq_stride) // tb + i, 0))
    out_spec = pl.BlockSpec((tb, d // HEAD, HEAD), lambda s, i: (s * (rows // tb) + i, 0, 0))
    return pl.pallas_call(
        _to_heads_kernel,
        grid=(n_seq, rows // tb),
        in_specs=[in_spec] * len(arrs),
        out_specs=[out_spec] * len(arrs),
        out_shape=[jax.ShapeDtypeStruct((n_seq * rows, d // HEAD, HEAD), a.dtype) for a in arrs],
        compiler_params=_params("parallel", "parallel"),
        name="to_heads",
    )(*arrs)


def _prompt_bias(table, qb):
    width = 3 * qb
    period = width + qb
    dist = (2 * qb - jnp.arange(period)) % period
    dist = jnp.where(dist > width, dist - period, dist)
    by_dist = table[:, jnp.clip(dist, -REL_CLIP, REL_CLIP) + REL_CLIP].astype(F32)
    heads = table.shape[0]
    rolled = jnp.tile(by_dist, (1, qb))[:, :qb * (period - 1)].reshape(heads, qb, period - 1)
    bias = rolled[:, :, :width]
    qi = jnp.arange(qb)[:, None]
    kj = jnp.arange(width)[None, :] - 2 * qb
    qc, kc = qi // CHUNK, jnp.floor_divide(kj, CHUNK)
    band = (kc <= qc) & (kc >= qc - PREV_CHUNKS)
    return jnp.where(band[None], bias, NEG_INF)


def _sample_bias(table, kb, t):
    dist = (kb + jnp.arange(t))[:, None] - jnp.arange(kb + t)[None, :]
    bias = table[:, jnp.clip(dist, -REL_CLIP, REL_CLIP) + REL_CLIP].astype(F32)
    return bias[:, :, :kb], bias[:, :, kb:]


def _swiglu_step(xb, w1, w3, w2):
    h1 = _dot(xb, w1)
    h3 = _dot(xb, w3)
    return _dot((h1 * jax.nn.sigmoid(h1) * h3).astype(BF16), w2)


def _dense_ffn_kernel(x_ref, w1_ref, w3_ref, w2_ref, g_ref, b_ref, o_ref, xb_ref, *, nf, alpha):
    f = pl.program_id(1)

    @pl.when(f == 0)
    def _():
        o_ref[...] = jnp.zeros(o_ref.shape, F32)
        xb_ref[...] = x_ref[...].astype(BF16)

    o_ref[...] += _swiglu_step(xb_ref[...], w1_ref[...].astype(BF16), w3_ref[...].astype(BF16),
                               w2_ref[...].astype(BF16))

    @pl.when(f == nf - 1)
    def _():
        o_ref[...] = _layer_norm(alpha * x_ref[...] + o_ref[...], g_ref[...], b_ref[...])


def _dense_ffn(x, w1, w3, w2, g, b, *, alpha, tm, tf=512):
    rows, d = x.shape
    dff = w1.shape[1]
    tf = _tile(dff, tf, LANES)
    nf = dff // tf
    row = pl.BlockSpec((tm, d), lambda t, f: (t, 0))
    vec = pl.BlockSpec((1, d), lambda t, f: (0, 0))
    w13 = pl.BlockSpec((d, tf), lambda t, f: (0, f))
    return pl.pallas_call(
        functools.partial(_dense_ffn_kernel, nf=nf, alpha=alpha),
        grid=(rows // tm, nf),
        in_specs=[row, w13, w13, pl.BlockSpec((tf, d), lambda t, f: (f, 0)), vec, vec],
        out_specs=row,
        out_shape=jax.ShapeDtypeStruct((rows, d), F32),
        scratch_shapes=[pltpu.VMEM((tm, d), BF16)],
        compiler_params=_params("parallel", "arbitrary"),
        name="dense_ffn",
    )(x, w1.astype(BF16), w3.astype(BF16), w2.astype(BF16), g, b)


def _router_kernel(x_ref, w_ref, o_ref, *, n_exp):
    x_hi, x_lo = _split_bf16(x_ref[...])
    w_hi, w_lo = _split_bf16(w_ref[...])
    logits = _dot(x_hi, w_hi) + _dot(x_hi, w_lo) + _dot(x_lo, w_hi)
    lane = lax.broadcasted_iota(jnp.int32, logits.shape, 1).astype(F32)
    valid = lane < n_exp
    lg = jnp.where(valid, logits, NEG_INF)
    ex = jnp.where(valid, jnp.exp(lg - jnp.max(lg, axis=-1, keepdims=True)), 0.0)
    probs = jnp.where(valid, ex / jnp.sum(ex, axis=-1, keepdims=True), -1.0)
    p1 = jnp.max(probs, axis=-1, keepdims=True)
    i1 = jnp.min(jnp.where(probs == p1, lane, float(LANES)), axis=-1, keepdims=True)
    rest = jnp.where(lane == i1, -1.0, probs)
    p2 = jnp.max(rest, axis=-1, keepdims=True)
    i2 = jnp.min(jnp.where(rest == p2, lane, float(LANES)), axis=-1, keepdims=True)
    tot = p1 + p2
    out = jnp.where(lane == 0, i1,
                    jnp.where(lane == 1, i2,
                              jnp.where(lane == 2, p1 / tot, jnp.where(lane == 3, p2 / tot, 0.0))))
    o_ref[...] = out


def _router(x, w, *, tm=512):
    m, d = x.shape
    n_exp = w.shape[1]
    wp = jnp.zeros((d, LANES), F32).at[:, :n_exp].set(w)
    tm = _tile(m, tm)
    return pl.pallas_call(
        functools.partial(_router_kernel, n_exp=n_exp),
        grid=(m // tm,),
        in_specs=[pl.BlockSpec((tm, d), lambda i: (i, 0)), pl.BlockSpec((d, LANES), lambda i: (0, 0))],
        out_specs=pl.BlockSpec((tm, LANES), lambda i: (i, 0)),
        out_shape=jax.ShapeDtypeStruct((m, LANES), F32),
        compiler_params=_params("parallel"),
        name="router",
    )(x, wp)


DMA_UNROLL = 8


def _row_copy(src_ref, dst_ref, sem, src_row, dst_row):
    return pltpu.make_async_copy(src_ref.at[pl.ds(src_row, 1)], dst_ref.at[pl.ds(dst_row, 1)], sem)


def _moe_ffn_kernel(te_ref, rows_ref, na_ref, src_ref, x_hbm, w1_ref, w3_ref, w2_ref, o_ref,
                    gbuf, xb_ref, sem, *, n_tiles, tm, sub, n_issue):
    t, f = pl.program_id(0), pl.program_id(1)
    rows = rows_ref[t]
    chunk = tm // n_issue

    def issue(tile, lo, n):
        def body(i, c):
            _row_copy(x_hbm, gbuf, sem, src_ref[tile * tm + lo + i], lo + i).start()
            return c
        lax.fori_loop(0, n, body, 0, unroll=DMA_UNROLL)

    def wait_tile():
        def body(i, c):
            _row_copy(x_hbm, gbuf, sem, 0, i).wait()
            return c
        lax.fori_loop(0, tm, body, 0, unroll=DMA_UNROLL)

    @pl.when(jnp.logical_and(jnp.logical_and(t == 0, f == 0), rows > 0))
    def _():
        issue(0, 0, tm)

    @pl.when(f == 0)
    def _():
        o_ref[...] = jnp.zeros(o_ref.shape, F32)

        @pl.when(rows > 0)
        def _():
            wait_tile()
            xb_ref[...] = gbuf[...].astype(BF16)

    nxt = jnp.minimum(t + 1, n_tiles - 1)
    fetch_next = jnp.logical_and(jnp.logical_and(f >= 1, f <= n_issue),
                                 jnp.logical_and(t + 1 < n_tiles, rows_ref[nxt] > 0))

    @pl.when(fetch_next)
    def _():
        issue(t + 1, (f - 1) * chunk, chunk)

    for n in range(sub, tm + 1, sub):
        @pl.when(jnp.logical_and(rows > n - sub, rows <= n))
        def _():
            o_ref[:n, :] += _swiglu_step(xb_ref[:n, :], w1_ref[...].astype(BF16),
                                         w3_ref[...].astype(BF16), w2_ref[...].astype(BF16))


def _moe_ffn(x, src, w1, w3, w2, layer, tile_expert, tile_rows, n_active, *, tm, tf):
    d = x.shape[1]
    n_slots = src.shape[0]
    n_tiles = n_slots // tm
    dff = w1.shape[3]
    tf = _tile(dff, min(tf, dff // 2), LANES)
    nf = dff // tf
    n_issue = 1
    while 2 * n_issue <= max(nf - 1, 1) and tm % (2 * n_issue) == 0:
        n_issue *= 2
    assert nf >= 2
    sub = LANES if tm % LANES == 0 else tm

    def clamp(t, na):
        return jnp.minimum(t, na[0] - 1)

    def f_idx(t, f, na):
        return jnp.where(t < na[0], f, nf - 1)

    w13 = pl.BlockSpec((None, None, d, tf),
                       lambda t, f, te, tr, na, sr: (layer, te[clamp(t, na)], 0, f_idx(t, f, na)))
    w2s = pl.BlockSpec((None, None, tf, d),
                       lambda t, f, te, tr, na, sr: (layer, te[clamp(t, na)], f_idx(t, f, na), 0))
    return pl.pallas_call(
        functools.partial(_moe_ffn_kernel, n_tiles=n_tiles, tm=tm, sub=sub, n_issue=n_issue),
        grid_spec=pltpu.PrefetchScalarGridSpec(
            num_scalar_prefetch=4,
            grid=(n_tiles, nf),
            in_specs=[pl.BlockSpec(memory_space=pl.ANY), w13, w13, w2s],
            out_specs=pl.BlockSpec((tm, d), lambda t, f, te, tr, na, sr: (t, 0),
                                   pipeline_mode=pl.Buffered(1)),
            scratch_shapes=[pltpu.VMEM((tm, d), F32), pltpu.VMEM((tm, d), BF16),
                            pltpu.SemaphoreType.DMA(())]),
        out_shape=jax.ShapeDtypeStruct((n_slots, d), F32),
        compiler_params=_params("arbitrary", "arbitrary"),
        name="moe_ffn",
    )(tile_expert, tile_rows, n_active, src, x, w1, w3, w2)


def _combine_kernel(i0_ref, i1_ref, y_ref, x_ref, sel_ref, g_ref, b_ref, o_ref, buf, sem, *, tc, alpha):
    base = pl.program_id(0) * tc

    def start(i, c):
        _row_copy(y_ref, buf.at[0], sem, i0_ref[base + i], i).start()
        _row_copy(y_ref, buf.at[1], sem, i1_ref[base + i], i).start()
        return c

    def wait(i, c):
        _row_copy(y_ref, buf.at[0], sem, 0, i).wait()
        _row_copy(y_ref, buf.at[1], sem, 0, i).wait()
        return c

    lax.fori_loop(0, tc, start, 0, unroll=DMA_UNROLL)
    lax.fori_loop(0, tc, wait, 0, unroll=DMA_UNROLL)
    sel = sel_ref[...]
    mixed = buf[0] * sel[:, TOP_K:TOP_K + 1] + buf[1] * sel[:, TOP_K + 1:TOP_K + 2]
    o_ref[...] = _layer_norm(alpha * x_ref[...] + mixed, g_ref[...], b_ref[...])


def _combine(y, x, sel, i0, i1, g, b, *, alpha, tc=256):
    m, d = x.shape
    tc = _tile(m, tc)
    row = pl.BlockSpec((tc, d), lambda i, a, c: (i, 0))
    vec = pl.BlockSpec((1, d), lambda i, a, c: (0, 0))
    return pl.pallas_call(
        functools.partial(_combine_kernel, tc=tc, alpha=alpha),
        grid_spec=pltpu.PrefetchScalarGridSpec(
            num_scalar_prefetch=2,
            grid=(m // tc,),
            in_specs=[pl.BlockSpec(memory_space=pl.ANY), row,
                      pl.BlockSpec((tc, LANES), lambda i, a, c: (i, 0)), vec, vec],
            out_specs=row,
            scratch_shapes=[pltpu.VMEM((2, tc, d), F32), pltpu.SemaphoreType.DMA(())]),
        out_shape=jax.ShapeDtypeStruct((m, d), F32),
        compiler_params=_params("arbitrary"),
        name="moe_combine",
    )(i0, i1, y, x, sel, g, b)


def _moe(x, router, w1, w3, w2, layer, g, b, *, alpha, tm):
    m, d = x.shape
    n_exp = w1.shape[1]
    sel = _router(x, router)
    top_i = sel[:, :TOP_K].astype(jnp.int32)
    e_flat = top_i.reshape(-1)
    onehot = (e_flat[:, None] == jnp.arange(n_exp)[None, :]).astype(jnp.int32)
    rank = jnp.sum((jnp.cumsum(onehot, axis=0) - onehot) * onehot, axis=1)
    counts = jnp.sum(onehot, axis=0)
    padded = (counts + tm - 1) // tm * tm
    ends = jnp.cumsum(padded)
    slot = jnp.sum(onehot * (ends - padded)[None, :], axis=1) + rank
    n_tiles = (TOP_K * m + n_exp * (tm - 1)) // tm
    n_slots = n_tiles * tm
    src = jnp.zeros((n_slots,), jnp.int32).at[slot].set(jnp.arange(TOP_K * m, dtype=jnp.int32) // TOP_K)
    tile_start = jnp.arange(n_tiles, dtype=jnp.int32) * tm
    tile_expert = jnp.minimum(jnp.sum((ends[None, :] <= tile_start[:, None]).astype(jnp.int32), axis=1),
                              n_exp - 1).astype(jnp.int32)
    group_fill = (ends - padded + counts).astype(jnp.int32)
    tile_rows = jnp.clip(group_fill[tile_expert] - tile_start, 0, tm).astype(jnp.int32)
    n_active = (ends[-1:] // tm).astype(jnp.int32)

    ys = _moe_ffn(x, src, w1, w3, w2, layer, tile_expert, tile_rows, n_active, tm=tm, tf=512)
    slot2 = slot.reshape(m, TOP_K).astype(jnp.int32)
    return _combine(ys, x, sel, slot2[:, 0], slot2[:, 1], g, b, alpha=alpha)


def kernel(x_prompt, x_sample, state_wkv, state_shift, cache_k, cache_v, ln_g, ln_b, rwkv_mu, rwkv_wr, rwkv_wk, rwkv_wv, rwkv_wo, rwkv_w0, rwkv_w1, rwkv_w2, rwkv_a0, rwkv_a1, rwkv_a2, rwkv_v0, rwkv_v1, rwkv_v2, rwkv_g1, rwkv_g2, rwkv_k_k, rwkv_k_a, rwkv_r_k, rwkv_lnx_g, rwkv_lnx_b, attn_wk, attn_wv, attn_wq, attn_wo, attn_rel_bias, ffn_w1, ffn_w3, ffn_w2, moe_router, moe_w1, moe_w3, moe_w2):
    bp, tp, d = x_prompt.shape
    bs, ts, _ = x_sample.shape
    depth = ln_g.shape[0]
    n_a = rwkv_wr.shape[0]
    alpha = (2 * depth) ** 0.25
    mp, ms = bp * tp, bs * ts
    m = mp + ms
    kb = cache_k.shape[1]
    assert tp % CHUNK == 0 and d % LANES == 0 and mp % ts == 0
    row = lambda a: a.reshape(1, -1)
    bf = lambda a: a.astype(BF16)
    qb = _tile(tp, 256, CHUNK)
    assert 2 * qb >= PREV
    tm_dense = _tile(m, 640)
    n_exp = moe_w1.shape[1]
    tm_moe = min(MOE_TILE_MAX, -(-int(TOP_K * m / n_exp / 2 * 1.05) // LANES) * LANES)
    tm_pre = _tile(math.gcd(mp, ms), 256)

    x = jnp.concatenate([x_prompt.reshape(mp, d), x_sample.reshape(ms, d)], axis=0)
    ridx = jnp.arange(m)
    start_flag = jnp.where(ridx < mp, (ridx % tp == 0) * 1.0, ((ridx - mp) % ts == 0) * 2.0)
    start_flag = start_flag.astype(F32).reshape(m, 1)

    p_wkv, s_wkv, p_shift, s_shift = [], [], [], []
    v_first = k_sh = v_sh = None
    for l in range(depth):
        g0, b0, g1, b1 = row(ln_g[l, 0]), row(ln_b[l, 0]), row(ln_g[l, 1]), row(ln_b[l, 1])
        if l < n_a:
            p_shift.append(x[tp - 1:mp:tp])
            s_shift.append(x[mp + ts - 1::ts])
            start_rows = jnp.repeat(state_shift[l], ts, axis=0)
            has_v = l > 0
            lora1 = [bf(rwkv_w1[l]), bf(rwkv_a1[l]), bf(rwkv_g1[l])] + ([bf(rwkv_v1[l - 1])] if has_v else [])
            lora2 = [bf(rwkv_w2[l]), bf(rwkv_a2[l]), bf(rwkv_g2[l])] + ([bf(rwkv_v2[l - 1])] if has_v else [])
            bias = [row(rwkv_w0[l]), row(rwkv_a0[l])] + ([row(rwkv_v0[l - 1])] if has_v else [])
            pre = _rwkv_pre(x, start_rows, start_flag, rwkv_mu[l], lora1, lora2, bias, tm=tm_pre)
            xr, xk, xv, lw, ag, gg = pre[:6]
            r = _matmul(xr, rwkv_wr, layer=l, name="rwkv_r")
            k = _matmul(xk, rwkv_wk, layer=l, name="rwkv_k")
            if has_v:
                v = _matmul(xv, rwkv_wv, layer=l, name="rwkv_v", extras=(v_first, pre[6]),
                            epilogue=lambda acc, vf, vg: acc + (vf - acc) * vg)
            else:
                v = _matmul(xv, rwkv_wv, layer=l, name="rwkv_v")
                v_first = v
            seqs = (r, k, v, lw, ag, gg)
            vecs = [row(rwkv_k_k[l]), row(rwkv_k_a[l]), row(rwkv_r_k[l]), row(rwkv_lnx_g[l]), row(rwkv_lnx_b[l])]
            zp, stp = _wkv(seqs, vecs, None, name="wkv_prompt", n_seq=bp, seq_len=tp, row0=0, L=CHUNK, hw=2048)
            zs, sts = _wkv(seqs, vecs, state_wkv[l].astype(F32), name="wkv_sample",
                           n_seq=bs, seq_len=ts, row0=mp, L=ts, hw=2048)
            p_wkv.append(stp.astype(state_wkv.dtype))
            s_wkv.append(sts.astype(state_wkv.dtype))
            z = jnp.concatenate([zp, zs], axis=0)
            x = _matmul(z, bf(rwkv_wo[l]), name="rwkv_out_ln", extras=(x,), vecs=(g0, b0),
                        epilogue=_ln_epilogue(alpha), tm=256, tn=d)
        else:
            i = l - n_a
            q = _matmul(x, attn_wq, layer=i, name="attn_q", out_dtype=BF16,
                        epilogue=lambda acc: acc * (HEAD ** -0.5))
            table = attn_rel_bias[i]
            op = _attn_prompt(q, k_sh, v_sh, _prompt_bias(table, qb), n_seq=bp, seq_len=tp, qb=qb)
            bias_c, bias_n = _sample_bias(table, kb, ts)
            os_ = _attn_sample(q, k_sh, v_sh, cache_k.reshape(bs, kb, d), cache_v.reshape(bs, kb, d),
                               bias_c, bias_n, n_seq=bs, seq_len=ts, row0=mp)
            o = jnp.concatenate([op, os_], axis=0)
            x = _matmul(o, bf(attn_wo[i]), name="attn_out_ln", extras=(x,), vecs=(g0, b0),
                        epilogue=_ln_epilogue(alpha), tm=256, tn=d)
        j = l // 2
        if l % 2 == 0:
            x = _dense_ffn(x, ffn_w1[j], ffn_w3[j], ffn_w2[j], g1, b1, alpha=alpha, tm=tm_dense)
        else:
            x = _moe(x, moe_router[j], moe_w1, moe_w3, moe_w2, j, g1, b1, alpha=alpha, tm=tm_moe)
        if l == n_a - 1:
            k_sh = _matmul(x, attn_wk, name="attn_k")
            v_sh = _matmul(x, attn_wv, name="attn_v")

    heads = d // HEAD
    keep = min(PREV, tp)
    kp, vp = _to_heads((k_sh, v_sh), row0=tp - keep, seq_stride=tp, n_seq=bp, rows=keep)
    ks, vs = _to_heads((k_sh, v_sh), row0=mp, seq_stride=ms, n_seq=1, rows=ms)
    return (x[:mp].reshape(bp, tp, d), x[mp:].reshape(bs, ts, d),
            jnp.stack(p_wkv), jnp.stack(p_shift),
            kp.reshape(bp, keep, heads, HEAD), vp.reshape(bp, keep, heads, HEAD),
            jnp.stack(s_wkv), jnp.stack(s_shift),
            ks.reshape(bs, ts, heads, HEAD), vs.reshape(bs, ts, heads, HEAD))
```

```python
import functools
import math

import jax
import jax.numpy as jnp
from jax import lax
from jax.experimental import pallas as pl
from jax.experimental.pallas import tpu as pltpu

F32 = jnp.float32
BF16 = jnp.bfloat16

CHUNK = 64
PREV_CHUNKS = 8
PREV = PREV_CHUNKS * CHUNK
REL_CLIP = 128
HEAD = 64
TOP_K = 2
GN_EPS = 64e-5
LN_EPS = 1e-5
NEG_INF = -1e30

LANES = 128
SUBLANES = 8
VMEM_LIMIT_BYTES = 60 * 1024 * 1024
MOE_TILE_MAX = 1280


def _params(*sem):
    return pltpu.CompilerParams(dimension_semantics=sem, vmem_limit_bytes=VMEM_LIMIT_BYTES)


def _tile(n, target, mult=16):
    best = None
    for t in range(mult, min(n, target) + 1, mult):
        if n % t == 0:
            best = t
    assert best is not None, (n, target, mult)
    return best


def _dot(a, b):
    return jnp.dot(a, b, preferred_element_type=F32)


def _dot_nt(a, b):
    return lax.dot_general(a, b, (((1,), (1,)), ((), ())), preferred_element_type=F32)


def _dot_tn(a, b):
    return lax.dot_general(a, b, (((0,), (0,)), ((), ())), preferred_element_type=F32)


def _split_bf16(x):
    hi = x.astype(BF16)
    lo = (x - hi.astype(F32)).astype(BF16)
    return hi, lo


def _layer_norm(y, g, b):
    mu = jnp.mean(y, axis=-1, keepdims=True)
    dev = y - mu
    var = jnp.mean(dev * dev, axis=-1, keepdims=True)
    return dev * lax.rsqrt(var + LN_EPS) * g + b


def _softplus(u):
    return jnp.maximum(u, 0.0) + jnp.log(1.0 + jnp.exp(-jnp.abs(u)))


def _mm_kernel(*refs, n_extra, n_vec, epilogue, cast_w):
    x_ref, w_ref = refs[0], refs[1]
    extra = refs[2:2 + n_extra]
    vecs = refs[2 + n_extra:2 + n_extra + n_vec]
    o_ref = refs[2 + n_extra + n_vec]
    if cast_w:
        wb_ref = refs[-1]

        @pl.when(pl.program_id(1) == 0)
        def _():
            wb_ref[...] = w_ref[...].astype(BF16)

        w = wb_ref[...]
    else:
        w = w_ref[...]
    acc = _dot(x_ref[...].astype(BF16), w)
    if epilogue is not None:
        acc = epilogue(acc, *[e[...] for e in extra], *[v[...] for v in vecs])
    o_ref[...] = acc.astype(o_ref.dtype)


def _matmul(x, w, *, name, layer=None, out_dtype=F32, epilogue=None, extras=(), vecs=(), tm=512, tn=1024):
    m, k = x.shape
    n = w.shape[-1]
    tm = _tile(m, tm)
    tn = n if n <= tn else _tile(n, tn, LANES)
    cast_w = w.dtype != BF16
    if w.ndim == 3:
        w_spec = pl.BlockSpec((None, k, tn), lambda j, i: (layer, 0, j))
    else:
        w_spec = pl.BlockSpec((k, tn), lambda j, i: (0, j))
    in_specs = [pl.BlockSpec((tm, k), lambda j, i: (i, 0)), w_spec]
    in_specs += [pl.BlockSpec((tm, tn), lambda j, i: (i, j)) for _ in extras]
    in_specs += [pl.BlockSpec((1, tn), lambda j, i: (0, j)) for _ in vecs]
    return pl.pallas_call(
        functools.partial(_mm_kernel, n_extra=len(extras), n_vec=len(vecs),
                          epilogue=epilogue, cast_w=cast_w),
        grid=(n // tn, m // tm),
        in_specs=in_specs,
        out_specs=pl.BlockSpec((tm, tn), lambda j, i: (i, j)),
        out_shape=jax.ShapeDtypeStruct((m, n), out_dtype),
        scratch_shapes=[pltpu.VMEM((k, tn), BF16)] if cast_w else [],
        compiler_params=_params("parallel", "arbitrary"),
        name=name,
    )(x, w, *extras, *vecs)


def _ln_epilogue(alpha):
    def epi(acc, xres, g, b):
        return _layer_norm(alpha * xres + acc, g, b)
    return epi


def _rwkv_pre_kernel(*refs, has_v):
    it = iter(refs)
    x_ref, halo_ref, start_ref, flag_ref, mu_ref = (next(it) for _ in range(5))
    w1, a1, g1 = next(it), next(it), next(it)
    v1 = next(it) if has_v else None
    w2, a2, g2 = next(it), next(it), next(it)
    v2 = next(it) if has_v else None
    w0, a0 = next(it), next(it)
    v0 = next(it) if has_v else None
    xr_ref, xk_ref, xv_ref, lw_ref, ag_ref, g_ref = (next(it) for _ in range(6))
    vg_ref = next(it) if has_v else None

    x = x_ref[...]
    row = lax.broadcasted_iota(jnp.int32, (x.shape[0], 1), 0)
    prev = jnp.where(row == 0, halo_ref[SUBLANES - 1:SUBLANES, :], pltpu.roll(x, 1, 0))
    flag = flag_ref[...]
    prev = jnp.where(flag == 1.0, 0.0, jnp.where(flag == 2.0, start_ref[...], prev))
    xx = prev - x
    mix = lambda i: (x + xx * mu_ref[i:i + 1, :]).astype(BF16)
    xr_ref[...] = mix(0)
    xk_ref[...] = mix(2)
    xv = mix(3)
    xv_ref[...] = xv
    hw = jnp.tanh(_dot(mix(1), w1[...]))
    z = w0[...] + _dot(hw.astype(BF16), w2[...])
    lw_ref[...] = -jnp.exp(-_softplus(-z) - 0.5)
    ha = _dot(mix(4), a1[...])
    ag_ref[...] = jax.nn.sigmoid(a0[...] + _dot(ha.astype(BF16), a2[...]))
    hg = jax.nn.sigmoid(_dot(mix(5), g1[...]))
    g_ref[...] = _dot(hg.astype(BF16), g2[...])
    if has_v:
        hv = _dot(xv, v1[...])
        vg_ref[...] = jax.nn.sigmoid(v0[...] + _dot(hv.astype(BF16), v2[...]))


def _rwkv_pre(x, start_rows, flag, mu, lora1, lora2, bias, *, tm):
    m, d = x.shape
    first_start_tile = (m - start_rows.shape[0]) // tm
    has_v = len(lora1) == 4
    row = pl.BlockSpec((tm, d), lambda i: (i, 0))
    halo = pl.BlockSpec((SUBLANES, d), lambda i: (jnp.maximum(i * (tm // SUBLANES) - 1, 0), 0))
    start = pl.BlockSpec((tm, d), lambda i: (jnp.maximum(i - first_start_tile, 0), 0))
    full = lambda a: pl.BlockSpec(a.shape, lambda i: (0,) * a.ndim)
    ins = [x, x, start_rows, flag, mu, *lora1, *lora2, *bias]
    in_specs = [row, halo, start, pl.BlockSpec((tm, 1), lambda i: (i, 0))] + [full(a) for a in ins[4:]]
    out_shape = [jax.ShapeDtypeStruct((m, d), BF16)] * 3 + [jax.ShapeDtypeStruct((m, d), F32)] * (3 + has_v)
    return pl.pallas_call(
        functools.partial(_rwkv_pre_kernel, has_v=has_v),
        grid=(m // tm,),
        in_specs=in_specs,
        out_specs=[row] * len(out_shape),
        out_shape=out_shape,
        compiler_params=_params("parallel"),
        name="rwkv_pre",
    )(*ins)


def _wkv_kernel(*refs, L, npair, nc, has_init):
    it = iter(refs)
    r_ref, k_ref, v_ref, lw_ref, ag_ref, g_ref = (next(it) for _ in range(6))
    kk_ref, ka_ref, rk_ref, lg_ref, lb_ref = (next(it) for _ in range(5))
    s0_ref = next(it) if has_init else None
    z_ref, so_ref, st_ref = next(it), next(it), next(it)
    L2 = 2 * L
    pairs = range(npair)
    bf = lambda t: t.astype(BF16)

    @pl.when(pl.program_id(2) == 0)
    def _():
        if has_init:
            zero = jnp.zeros((HEAD, HEAD), F32)
            for p in pairs:
                top = jnp.concatenate([s0_ref[0, 2 * p], zero], axis=1)
                bot = jnp.concatenate([zero, s0_ref[0, 2 * p + 1]], axis=1)
                st_ref[p] = jnp.concatenate([top, bot], axis=0).T
        else:
            st_ref[...] = jnp.zeros(st_ref.shape, F32)

    def iota(shape, dim):
        return lax.broadcasted_iota(jnp.int32, shape, dim)

    tri = (iota((L, L), 0) >= iota((L, L), 1)).astype(BF16)
    lw = lw_ref[...]
    lw_hi, lw_lo = _split_bf16(lw)
    cum = _dot(tri, lw_hi) + _dot(tri, lw_lo)
    cum_last = cum[L - 1:L, :]
    e_pos = jnp.exp(cum)
    e_neg = jnp.exp(-cum)
    e_prev = jnp.exp(cum - lw)
    e_last = jnp.exp(cum_last - cum)
    lw_split = jnp.concatenate([lw_hi, lw_lo], axis=0)

    rows, cols = iota((L2, L2), 0), iota((L2, L2), 1)
    same_head = (rows >= L) == (cols >= L)
    lower_strict = same_head & (rows > cols)
    lower_incl = same_head & (rows >= cols)
    eye = (rows == cols).astype(F32)
    own_lanes = ((iota((L2, LANES), 0) >= L) == (iota((L2, LANES), 1) >= HEAD)).astype(F32)
    ones = jnp.ones((L2, LANES), BF16)
    dup = lambda t: jnp.concatenate([t, t], axis=0)
    stack = lambda t: dup(t) * own_lanes
    unstack = lambda t: t[:L] + t[L:]
    lanes = lambda p: slice(p * LANES, (p + 1) * LANES)
    left, right = slice(0, LANES), slice(LANES, 2 * LANES)

    ops = []
    for p in pairs:
        sl = lanes(p)
        ag = ag_ref[:, sl]
        kraw = k_ref[:, sl]
        kk = stack(kraw * kk_ref[:, sl])
        kk = kk / jnp.maximum(jnp.sqrt(jnp.sum(kk * kk, axis=-1, keepdims=True)), 1e-12)
        ks = stack(kraw * (1.0 + (ag - 1.0) * ka_ref[:, sl]))
        rs = stack(r_ref[:, sl])
        vs = stack(v_ref[:, sl])
        b = kk * dup(ag)
        en = dup(e_neg[:, sl])
        el = dup(e_last[:, sl])
        ops.append(dict(
            ks=ks, rs=rs, vs=vs, vb=bf(vs),
            at=bf(-kk * dup(e_prev[:, sl])), rt=rs * dup(e_pos[:, sl]),
            bt=bf(b * en), kt=bf(ks * en), bh=bf(b * el), kh=bf(ks * el)))

    for o in ops:
        o["rtb"] = bf(o["rt"])
        if L2 % LANES == 0:
            m = _dot_nt(jnp.concatenate([o["at"], o["rtb"]], axis=0),
                        jnp.concatenate([o["bt"], o["kt"]], axis=0))
            m_ab, m_ak, m_rb, m_rk = m[:L2, :L2], m[:L2, L2:], m[L2:, :L2], m[L2:, L2:]
        else:
            m_ab, m_ak = _dot_nt(o["at"], o["bt"]), _dot_nt(o["at"], o["kt"])
            m_rb, m_rk = _dot_nt(o["rtb"], o["bt"]), _dot_nt(o["rtb"], o["kt"])
        o["m_ab"] = jnp.where(lower_strict, m_ab, 0.0)
        o["m_ak"] = bf(jnp.where(lower_strict, m_ak, 0.0))
        o["m_rb"] = bf(jnp.where(lower_incl, m_rb, 0.0))
        o["m_rk"] = bf(jnp.where(lower_incl, m_rk, 0.0))

    for o in ops:
        o["tinv"] = eye + o["m_ab"]
        o["mp"] = o["m_ab"]
    span = 2
    while span < L:
        for o in ops:
            mpb = bf(o["mp"])
            o["mp"] = _dot(mpb, mpb)
        for o in ops:
            o["tinv"] = o["tinv"] + _dot(bf(o["tinv"]), bf(o["mp"]))
        span *= 2

    for o in ops:
        o["makv"] = bf(_dot(o["m_ak"], o["vb"]))
    for o in ops:
        au = _dot(bf(o["tinv"]), jnp.concatenate([o["at"], o["makv"]], axis=1))
        o["aub"] = bf(au)
    for o in ops:
        ry = _dot(o["m_rb"], o["aub"])
        o["rh"] = bf(o["rt"] + ry[:, left])
        o["y0"] = ry[:, right] + _dot(o["m_rk"], o["vb"])
        gq = _dot_tn(o["bh"], o["aub"])
        o["g"] = bf(gq[:, left])
        o["q"] = gq[:, right] + _dot_tn(o["kh"], o["vb"])
    last = pl.program_id(2) == nc - 1
    for p, o in zip(pairs, ops):
        decay = jnp.exp(_dot_tn(lw_split[:, lanes(p)], ones))
        st = st_ref[p]
        stb = bf(st)
        o["ys"] = _dot(o["rh"], stb) + o["y0"]
        st_ref[p] = decay * st + _dot(o["g"], stb) + o["q"]

    @pl.when(last)
    def _():
        for p in pairs:
            t = st_ref[p].T
            so_ref[0, 2 * p] = t[:HEAD, :HEAD]
            so_ref[0, 2 * p + 1] = t[HEAD:, HEAD:]

    for p, o in zip(pairs, ops):
        sl = lanes(p)
        ys = o["ys"]
        mean = jnp.sum(ys, axis=-1, keepdims=True) * (1.0 / HEAD)
        dev = (ys - mean) * own_lanes
        var = jnp.sum(dev * dev, axis=-1, keepdims=True) * (1.0 / HEAD)
        yn = unstack(dev * lax.rsqrt(var + GN_EPS)) * lg_ref[:, sl] + lb_ref[:, sl]
        bonus = unstack(jnp.sum(o["rs"] * o["ks"] * rk_ref[:, sl], axis=-1, keepdims=True) * o["vs"])
        z_ref[:, sl] = ((yn + bonus) * g_ref[:, sl]).astype(z_ref.dtype)


def _wkv(seqs, vecs, s0, *, name, n_seq, seq_len, row0, L, hw):
    d = seqs[0].shape[1]
    hw = min(hw, d)
    nc = seq_len // L
    blk0 = row0 // L
    npair = hw // LANES
    seq_spec = pl.BlockSpec((L, hw), lambda b, h, c: (blk0 + b * nc + c, h))
    vec_spec = pl.BlockSpec((1, hw), lambda b, h, c: (0, h))
    st_spec = pl.BlockSpec((1, 2 * npair, HEAD, HEAD), lambda b, h, c: (b, h, 0, 0))
    in_specs = [seq_spec] * 6 + [vec_spec] * 5 + ([st_spec] if s0 is not None else [])
    ins = list(seqs) + list(vecs) + ([s0] if s0 is not None else [])
    return pl.pallas_call(
        functools.partial(_wkv_kernel, L=L, npair=npair, nc=nc, has_init=s0 is not None),
        grid=(n_seq, d // hw, nc),
        in_specs=in_specs,
        out_specs=[pl.BlockSpec((L, hw), lambda b, h, c: (b * nc + c, h)), st_spec],
        out_shape=[jax.ShapeDtypeStruct((n_seq * seq_len, d), BF16),
                   jax.ShapeDtypeStruct((n_seq, d // HEAD, HEAD, HEAD), F32)],
        scratch_shapes=[pltpu.VMEM((npair, LANES, LANES), F32)],
        compiler_params=_params("parallel", "parallel", "arbitrary"),
        name=name,
    )(*ins)


def _head_masks():
    lane = lax.broadcasted_iota(jnp.int32, (1, LANES), 1)
    return lane < HEAD


def _attn_prompt_kernel(q_ref, k0, k1, k2, v0, v1, v2, bias_ref, o_ref, *, qb, npair):
    first = _head_masks()
    blk = pl.program_id(2)
    lanes = lambda p: slice(p * LANES, (p + 1) * LANES)
    cat = lambda refs, p: jnp.concatenate([r[:, lanes(p)] for r in refs], axis=0).astype(BF16)
    kpos = lax.broadcasted_iota(jnp.int32, (1, 3 * qb), 1) + (blk - 2) * qb
    valid = kpos >= 0
    units = []
    for p in range(npair):
        q = q_ref[:, lanes(p)]
        kcat = cat((k0, k1, k2), p)
        for h in range(2):
            qh = jnp.where(first if h == 0 else ~first, q, 0.0).astype(BF16)
            units.append(dict(p=p, h=2 * p + h, s=_dot_nt(qh, kcat)))
    for u in units:
        s = jnp.where(valid, u["s"] + bias_ref[u["h"]], NEG_INF)
        e = jnp.exp(s - jnp.max(s, axis=-1, keepdims=True))
        u["denom"] = jnp.sum(e, axis=-1, keepdims=True)
        u["e"] = e.astype(BF16)
    for p in range(npair):
        vcat = cat((v0, v1, v2), p)
        outs = [_dot(u["e"], vcat) / u["denom"] for u in units[2 * p:2 * p + 2]]
        o_ref[:, lanes(p)] = jnp.where(first, outs[0], outs[1]).astype(o_ref.dtype)


def _attn_prompt(q, k, v, bias, *, n_seq, seq_len, qb, hw=4 * LANES):
    d = q.shape[1]
    nq = seq_len // qb
    hw = min(hw, d)
    npair = hw // LANES
    q_spec = pl.BlockSpec((qb, hw), lambda h, b, i: (b * nq + i, h))
    kv = lambda off: pl.BlockSpec((qb, hw), lambda h, b, i: (b * nq + jnp.maximum(i - off, 0), h))
    return pl.pallas_call(
        functools.partial(_attn_prompt_kernel, qb=qb, npair=npair),
        grid=(d // hw, n_seq, nq),
        in_specs=[q_spec, kv(2), kv(1), kv(0), kv(2), kv(1), kv(0),
                  pl.BlockSpec((2 * npair, qb, 3 * qb), lambda h, b, i: (h, 0, 0))],
        out_specs=pl.BlockSpec((qb, hw), lambda h, b, i: (b * nq + i, h)),
        out_shape=jax.ShapeDtypeStruct((n_seq * seq_len, d), BF16),
        compiler_params=_params("parallel", "parallel", "arbitrary"),
        name="attn_prompt",
    )(q, k, k, k, v, v, v, bias)


def _attn_sample_kernel(q_ref, kn_ref, vn_ref, kc_ref, vc_ref, bc_ref, bn_ref, o_ref, *, npair):
    first = _head_masks()
    lanes = lambda p: slice(p * LANES, (p + 1) * LANES)
    units = []
    for p in range(npair):
        q = q_ref[:, lanes(p)]
        kc, kn = kc_ref[0, :, lanes(p)].astype(BF16), kn_ref[:, lanes(p)].astype(BF16)
        for h in range(2):
            qh = jnp.where(first if h == 0 else ~first, q, 0.0).astype(BF16)
            units.append(dict(p=p, h=2 * p + h, sc=_dot_nt(qh, kc), sn=_dot_nt(qh, kn)))
    for u in units:
        sc = u["sc"] + bc_ref[u["h"]]
        sn = u["sn"] + bn_ref[u["h"]]
        mx = jnp.maximum(jnp.max(sc, axis=-1, keepdims=True), jnp.max(sn, axis=-1, keepdims=True))
        pc, pn = jnp.exp(sc - mx), jnp.exp(sn - mx)
        u["denom"] = jnp.sum(pc, axis=-1, keepdims=True) + jnp.sum(pn, axis=-1, keepdims=True)
        u["pc"], u["pn"] = pc.astype(BF16), pn.astype(BF16)
    for u in units:
        vc, vn = vc_ref[0, :, lanes(u["p"])].astype(BF16), vn_ref[:, lanes(u["p"])].astype(BF16)
        u["o"] = (_dot(u["pc"], vc) + _dot(u["pn"], vn)) / u["denom"]
    for p in range(npair):
        o_ref[:, lanes(p)] = jnp.where(first, units[2 * p]["o"], units[2 * p + 1]["o"]).astype(o_ref.dtype)


def _attn_sample(q, k, v, cache_k, cache_v, bias_c, bias_n, *, n_seq, seq_len, row0):
    d = q.shape[1]
    kb = cache_k.shape[1]
    blk0 = row0 // seq_len
    new = pl.BlockSpec((seq_len, d), lambda s: (blk0 + s, 0))
    cache = pl.BlockSpec((1, kb, d), lambda s: (s, 0, 0))
    full = lambda a: pl.BlockSpec(a.shape, lambda s: (0,) * a.ndim)
    return pl.pallas_call(
        functools.partial(_attn_sample_kernel, npair=d // LANES),
        grid=(n_seq,),
        in_specs=[new, new, new, cache, cache, full(bias_c), full(bias_n)],
        out_specs=pl.BlockSpec((seq_len, d), lambda s: (s, 0)),
        out_shape=jax.ShapeDtypeStruct((n_seq * seq_len, d), BF16),
        compiler_params=_params("parallel"),
        name="attn_sample",
    )(q, k, v, cache_k, cache_v, bias_c, bias_n)


def _to_heads_kernel(*refs):
    n = len(refs) // 2
    for x_ref, o_ref in zip(refs[:n], refs[n:]):
        o_ref[...] = x_ref[...].reshape(o_ref.shape)


def _to_heads(arrs, *, row0, seq_stride, n_seq, rows):
    d = arrs[0].shape[1]
    tb = _tile(math.gcd(math.gcd(row0, seq_stride), rows), LANES, SUBLANES)
    in_spec = pl.BlockSpec((tb, d), lambda s, i: ((row0 + s * seq_stride) // tb + i, 0))
    out_spec = pl.BlockSpec((tb, d // HEAD, HEAD), lambda s, i: (s * (rows // tb) + i, 0, 0))
    return pl.pallas_call(
        _to_heads_kernel,
        grid=(n_seq, rows // tb),
        in_specs=[in_spec] * len(arrs),
        out_specs=[out_spec] * len(arrs),
        out_shape=[jax.ShapeDtypeStruct((n_seq * rows, d // HEAD, HEAD), a.dtype) for a in arrs],
        compiler_params=_params("parallel", "parallel"),
        name="to_heads",
    )(*arrs)


def _prompt_bias(table, qb):
    width = 3 * qb
    period = width + qb
    dist = (2 * qb - jnp.arange(period)) % period
    dist = jnp.where(dist > width, dist - period, dist)
    by_dist = table[:, jnp.clip(dist, -REL_CLIP, REL_CLIP) + REL_CLIP].astype(F32)
    heads = table.shape[0]
    rolled = jnp.tile(by_dist, (1, qb))[:, :qb * (period - 1)].reshape(heads, qb, period - 1)
    bias = rolled[:, :, :width]
    qi = jnp.arange(qb)[:, None]
    kj = jnp.arange(width)[None, :] - 2 * qb
    qc, kc = qi // CHUNK, jnp.floor_divide(kj, CHUNK)
    band = (kc <= qc) & (kc >= qc - PREV_CHUNKS)
    return jnp.where(band[None], bias, NEG_INF)


def _sample_bias(table, kb, t):
    dist = (kb + jnp.arange(t))[:, None] - jnp.arange(kb + t)[None, :]
    bias = table[:, jnp.clip(dist, -REL_CLIP, REL_CLIP) + REL_CLIP].astype(F32)
    return bias[:, :, :kb], bias[:, :, kb:]


def _swiglu_step(xb, w1, w3, w2):
    h1 = _dot(xb, w1)
    h3 = _dot(xb, w3)
    return _dot((h1 * jax.nn.sigmoid(h1) * h3).astype(BF16), w2)


def _dense_ffn_kernel(x_ref, w1_ref, w3_ref, w2_ref, g_ref, b_ref, o_ref, xb_ref, *, nf, alpha):
    f = pl.program_id(1)

    @pl.when(f == 0)
    def _():
        o_ref[...] = jnp.zeros(o_ref.shape, F32)
        xb_ref[...] = x_ref[...].astype(BF16)

    o_ref[...] += _swiglu_step(xb_ref[...], w1_ref[...].astype(BF16), w3_ref[...].astype(BF16),
                               w2_ref[...].astype(BF16))

    @pl.when(f == nf - 1)
    def _():
        o_ref[...] = _layer_norm(alpha * x_ref[...] + o_ref[...], g_ref[...], b_ref[...])


def _dense_ffn(x, w1, w3, w2, g, b, *, alpha, tm, tf=512):
    rows, d = x.shape
    dff = w1.shape[1]
    tf = _tile(dff, tf, LANES)
    nf = dff // tf
    row = pl.BlockSpec((tm, d), lambda t, f: (t, 0))
    vec = pl.BlockSpec((1, d), lambda t, f: (0, 0))
    w13 = pl.BlockSpec((d, tf), lambda t, f: (0, f))
    return pl.pallas_call(
        functools.partial(_dense_ffn_kernel, nf=nf, alpha=alpha),
        grid=(rows // tm, nf),
        in_specs=[row, w13, w13, pl.BlockSpec((tf, d), lambda t, f: (f, 0)), vec, vec],
        out_specs=row,
        out_shape=jax.ShapeDtypeStruct((rows, d), F32),
        scratch_shapes=[pltpu.VMEM((tm, d), BF16)],
        compiler_params=_params("parallel", "arbitrary"),
        name="dense_ffn",
    )(x, w1.astype(BF16), w3.astype(BF16), w2.astype(BF16), g, b)


def _router_kernel(x_ref, w_ref, o_ref, *, n_exp):
    x_hi, x_lo = _split_bf16(x_ref[...])
    w_hi, w_lo = _split_bf16(w_ref[...])
    logits = _dot(x_hi, w_hi) + _dot(x_hi, w_lo) + _dot(x_lo, w_hi)
    lane = lax.broadcasted_iota(jnp.int32, logits.shape, 1).astype(F32)
    valid = lane < n_exp
    lg = jnp.where(valid, logits, NEG_INF)
    ex = jnp.where(valid, jnp.exp(lg - jnp.max(lg, axis=-1, keepdims=True)), 0.0)
    probs = jnp.where(valid, ex / jnp.sum(ex, axis=-1, keepdims=True), -1.0)
    p1 = jnp.max(probs, axis=-1, keepdims=True)
    i1 = jnp.min(jnp.where(probs == p1, lane, float(LANES)), axis=-1, keepdims=True)
    rest = jnp.where(lane == i1, -1.0, probs)
    p2 = jnp.max(rest, axis=-1, keepdims=True)
    i2 = jnp.min(jnp.where(rest == p2, lane, float(LANES)), axis=-1, keepdims=True)
    tot = p1 + p2
    out = jnp.where(lane == 0, i1,
                    jnp.where(lane == 1, i2,
                              jnp.where(lane == 2, p1 / tot, jnp.where(lane == 3, p2 / tot, 0.0))))
    o_ref[...] = out


def _router(x, w, *, tm=512):
    m, d = x.shape
    n_exp = w.shape[1]
    wp = jnp.zeros((d, LANES), F32).at[:, :n_exp].set(w)
    tm = _tile(m, tm)
    return pl.pallas_call(
        functools.partial(_router_kernel, n_exp=n_exp),
        grid=(m // tm,),
        in_specs=[pl.BlockSpec((tm, d), lambda i: (i, 0)), pl.BlockSpec((d, LANES), lambda i: (0, 0))],
        out_specs=pl.BlockSpec((tm, LANES), lambda i: (i, 0)),
        out_shape=jax.ShapeDtypeStruct((m, LANES), F32),
        compiler_params=_params("parallel"),
        name="router",
    )(x, wp)


DMA_UNROLL = 8


def _row_copy(src_ref, dst_ref, sem, src_row, dst_row):
    return pltpu.make_async_copy(src_ref.at[pl.ds(src_row, 1)], dst_ref.at[pl.ds(dst_row, 1)], sem)


def _moe_ffn_kernel(te_ref, rows_ref, na_ref, src_ref, x_hbm, w1_ref, w3_ref, w2_ref, o_ref,
                    gbuf, xb_ref, sem, *, n_tiles, tm, sub, n_issue):
    t, f = pl.program_id(0), pl.program_id(1)
    rows = rows_ref[t]
    chunk = tm // n_issue

    def issue(tile, lo, n):
        def body(i, c):
            _row_copy(x_hbm, gbuf, sem, src_ref[tile * tm + lo + i], lo + i).start()
            return c
        lax.fori_loop(0, n, body, 0, unroll=DMA_UNROLL)

    def wait_tile():
        def body(i, c):
            _row_copy(x_hbm, gbuf, sem, 0, i).wait()
            return c
        lax.fori_loop(0, tm, body, 0, unroll=DMA_UNROLL)

    @pl.when(jnp.logical_and(jnp.logical_and(t == 0, f == 0), rows > 0))
    def _():
        issue(0, 0, tm)

    @pl.when(f == 0)
    def _():
        o_ref[...] = jnp.zeros(o_ref.shape, F32)

        @pl.when(rows > 0)
        def _():
            wait_tile()
            xb_ref[...] = gbuf[...].astype(BF16)

    nxt = jnp.minimum(t + 1, n_tiles - 1)
    fetch_next = jnp.logical_and(jnp.logical_and(f >= 1, f <= n_issue),
                                 jnp.logical_and(t + 1 < n_tiles, rows_ref[nxt] > 0))

    @pl.when(fetch_next)
    def _():
        issue(t + 1, (f - 1) * chunk, chunk)

    for n in range(sub, tm + 1, sub):
        @pl.when(jnp.logical_and(rows > n - sub, rows <= n))
        def _():
            o_ref[:n, :] += _swiglu_step(xb_ref[:n, :], w1_ref[...].astype(BF16),
                                         w3_ref[...].astype(BF16), w2_ref[...].astype(BF16))


def _moe_ffn(x, src, w1, w3, w2, layer, tile_expert, tile_rows, n_active, *, tm, tf):
    d = x.shape[1]
    n_slots = src.shape[0]
    n_tiles = n_slots // tm
    dff = w1.shape[3]
    tf = _tile(dff, min(tf, dff // 2), LANES)
    nf = dff // tf
    n_issue = 1
    while 2 * n_issue <= max(nf - 1, 1) and tm % (2 * n_issue) == 0:
        n_issue *= 2
    assert nf >= 2
    sub = LANES if tm % LANES == 0 else tm

    def clamp(t, na):
        return jnp.minimum(t, na[0] - 1)

    def f_idx(t, f, na):
        return jnp.where(t < na[0], f, nf - 1)

    w13 = pl.BlockSpec((None, None, d, tf),
                       lambda t, f, te, tr, na, sr: (layer, te[clamp(t, na)], 0, f_idx(t, f, na)))
    w2s = pl.BlockSpec((None, None, tf, d),
                       lambda t, f, te, tr, na, sr: (layer, te[clamp(t, na)], f_idx(t, f, na), 0))
    return pl.pallas_call(
        functools.partial(_moe_ffn_kernel, n_tiles=n_tiles, tm=tm, sub=sub, n_issue=n_issue),
        grid_spec=pltpu.PrefetchScalarGridSpec(
            num_scalar_prefetch=4,
            grid=(n_tiles, nf),
            in_specs=[pl.BlockSpec(memory_space=pl.ANY), w13, w13, w2s],
            out_specs=pl.BlockSpec((tm, d), lambda t, f, te, tr, na, sr: (t, 0),
                                   pipeline_mode=pl.Buffered(1)),
            scratch_shapes=[pltpu.VMEM((tm, d), F32), pltpu.VMEM((tm, d), BF16),
                            pltpu.SemaphoreType.DMA(())]),
        out_shape=jax.ShapeDtypeStruct((n_slots, d), F32),
        compiler_params=_params("arbitrary", "arbitrary"),
        name="moe_ffn",
    )(tile_expert, tile_rows, n_active, src, x, w1, w3, w2)


def _combine_kernel(i0_ref, i1_ref, y_ref, x_ref, sel_ref, g_ref, b_ref, o_ref, buf, sem, *, tc, alpha):
    base = pl.program_id(0) * tc

    def start(i, c):
        _row_copy(y_ref, buf.at[0], sem, i0_ref[base + i], i).start()
        _row_copy(y_ref, buf.at[1], sem, i1_ref[base + i], i).start()
        return c

    def wait(i, c):
        _row_copy(y_ref, buf.at[0], sem, 0, i).wait()
        _row_copy(y_ref, buf.at[1], sem, 0, i).wait()
        return c

    lax.fori_loop(0, tc, start, 0, unroll=DMA_UNROLL)
    lax.fori_loop(0, tc, wait, 0, unroll=DMA_UNROLL)
    sel = sel_ref[...]
    mixed = buf[0] * sel[:, TOP_K:TOP_K + 1] + buf[1] * sel[:, TOP_K + 1:TOP_K + 2]
    o_ref[...] = _layer_norm(alpha * x_ref[...] + mixed, g_ref[...], b_ref[...])


def _combine(y, x, sel, i0, i1, g, b, *, alpha, tc=256):
    m, d = x.shape
    tc = _tile(m, tc)
    row = pl.BlockSpec((tc, d), lambda i, a, c: (i, 0))
    vec = pl.BlockSpec((1, d), lambda i, a, c: (0, 0))
    return pl.pallas_call(
        functools.partial(_combine_kernel, tc=tc, alpha=alpha),
        grid_spec=pltpu.PrefetchScalarGridSpec(
            num_scalar_prefetch=2,
            grid=(m // tc,),
            in_specs=[pl.BlockSpec(memory_space=pl.ANY), row,
                      pl.BlockSpec((tc, LANES), lambda i, a, c: (i, 0)), vec, vec],
            out_specs=row,
            scratch_shapes=[pltpu.VMEM((2, tc, d), F32), pltpu.SemaphoreType.DMA(())]),
        out_shape=jax.ShapeDtypeStruct((m, d), F32),
        compiler_params=_params("arbitrary"),
        name="moe_combine",
    )(i0, i1, y, x, sel, g, b)


def _moe(x, router, w1, w3, w2, layer, g, b, *, alpha, tm):
    m, d = x.shape
    n_exp = w1.shape[1]
    sel = _router(x, router)
    top_i = sel[:, :TOP_K].astype(jnp.int32)
    e_flat = top_i.reshape(-1)
    onehot = (e_flat[:, None] == jnp.arange(n_exp)[None, :]).astype(jnp.int32)
    rank = jnp.sum((jnp.cumsum(onehot, axis=0) - onehot) * onehot, axis=1)
    counts = jnp.sum(onehot, axis=0)
    padded = (counts + tm - 1) // tm * tm
    ends = jnp.cumsum(padded)
    slot = jnp.sum(onehot * (ends - padded)[None, :], axis=1) + rank
    n_tiles = (TOP_K * m + n_exp * (tm - 1)) // tm
    n_slots = n_tiles * tm
    src = jnp.zeros((n_slots,), jnp.int32).at[slot].set(jnp.arange(TOP_K * m, dtype=jnp.int32) // TOP_K)
    tile_start = jnp.arange(n_tiles, dtype=jnp.int32) * tm
    tile_expert = jnp.minimum(jnp.sum((ends[None, :] <= tile_start[:, None]).astype(jnp.int32), axis=1),
                              n_exp - 1).astype(jnp.int32)
    group_fill = (ends - padded + counts).astype(jnp.int32)
    tile_rows = jnp.clip(group_fill[tile_expert] - tile_start, 0, tm).astype(jnp.int32)
    n_active = (ends[-1:] // tm).astype(jnp.int32)

    ys = _moe_ffn(x, src, w1, w3, w2, layer, tile_expert, tile_rows, n_active, tm=tm, tf=512)
    slot2 = slot.reshape(m, TOP_K).astype(jnp.int32)
    return _combine(ys, x, sel, slot2[:, 0], slot2[:, 1], g, b, alpha=alpha)


def kernel(x_prompt, x_sample, state_wkv, state_shift, cache_k, cache_v, ln_g, ln_b, rwkv_mu, rwkv_wr, rwkv_wk, rwkv_wv, rwkv_wo, rwkv_w0, rwkv_w1, rwkv_w2, rwkv_a0, rwkv_a1, rwkv_a2, rwkv_v0, rwkv_v1, rwkv_v2, rwkv_g1, rwkv_g2, rwkv_k_k, rwkv_k_a, rwkv_r_k, rwkv_lnx_g, rwkv_lnx_b, attn_wk, attn_wv, attn_wq, attn_wo, attn_rel_bias, ffn_w1, ffn_w3, ffn_w2, moe_router, moe_w1, moe_w3, moe_w2):
    bp, tp, d = x_prompt.shape
    bs, ts, _ = x_sample.shape
    depth = ln_g.shape[0]
    n_a = rwkv_wr.shape[0]
    alpha = (2 * depth) ** 0.25
    mp, ms = bp * tp, bs * ts
    m = mp + ms
    kb = cache_k.shape[1]
    assert tp % CHUNK == 0 and d % LANES == 0 and mp % ts == 0
    row = lambda a: a.reshape(1, -1)
    bf = lambda a: a.astype(BF16)
    qb = _tile(tp, 256, CHUNK)
    assert 2 * qb >= PREV
    tm_dense = _tile(m, 640)
    n_exp = moe_w1.shape[1]
    tm_moe = min(MOE_TILE_MAX, -(-int(TOP_K * m / n_exp / 2 * 1.15) // LANES) * LANES)
    tm_pre = _tile(math.gcd(mp, ms), 256)

    x = jnp.concatenate([x_prompt.reshape(mp, d), x_sample.reshape(ms, d)], axis=0)
    ridx = jnp.arange(m)
    start_flag = jnp.where(ridx < mp, (ridx % tp == 0) * 1.0, ((ridx - mp) % ts == 0) * 2.0)
    start_flag = start_flag.astype(F32).reshape(m, 1)

    p_wkv, s_wkv, p_shift, s_shift = [], [], [], []
    v_first = k_sh = v_sh = None
    for l in range(depth):
        g0, b0, g1, b1 = row(ln_g[l, 0]), row(ln_b[l, 0]), row(ln_g[l, 1]), row(ln_b[l, 1])
        if l < n_a:
            p_shift.append(x[tp - 1:mp:tp])
            s_shift.append(x[mp + ts - 1::ts])
            start_rows = jnp.repeat(state_shift[l], ts, axis=0)
            has_v = l > 0
            lora1 = [bf(rwkv_w1[l]), bf(rwkv_a1[l]), bf(rwkv_g1[l])] + ([bf(rwkv_v1[l - 1])] if has_v else [])
            lora2 = [bf(rwkv_w2[l]), bf(rwkv_a2[l]), bf(rwkv_g2[l])] + ([bf(rwkv_v2[l - 1])] if has_v else [])
            bias = [row(rwkv_w0[l]), row(rwkv_a0[l])] + ([row(rwkv_v0[l - 1])] if has_v else [])
            pre = _rwkv_pre(x, start_rows, start_flag, rwkv_mu[l], lora1, lora2, bias, tm=tm_pre)
            xr, xk, xv, lw, ag, gg = pre[:6]
            r = _matmul(xr, rwkv_wr, layer=l, name="rwkv_r")
            k = _matmul(xk, rwkv_wk, layer=l, name="rwkv_k")
            if has_v:
                v = _matmul(xv, rwkv_wv, layer=l, name="rwkv_v", extras=(v_first, pre[6]),
                            epilogue=lambda acc, vf, vg: acc + (vf - acc) * vg)
            else:
                v = _matmul(xv, rwkv_wv, layer=l, name="rwkv_v")
                v_first = v
            seqs = (r, k, v, lw, ag, gg)
            vecs = [row(rwkv_k_k[l]), row(rwkv_k_a[l]), row(rwkv_r_k[l]), row(rwkv_lnx_g[l]), row(rwkv_lnx_b[l])]
            zp, stp = _wkv(seqs, vecs, None, name="wkv_prompt", n_seq=bp, seq_len=tp, row0=0, L=CHUNK, hw=2048)
            zs, sts = _wkv(seqs, vecs, state_wkv[l].astype(F32), name="wkv_sample",
                           n_seq=bs, seq_len=ts, row0=mp, L=ts, hw=2048)
            p_wkv.append(stp.astype(state_wkv.dtype))
            s_wkv.append(sts.astype(state_wkv.dtype))
            z = jnp.concatenate([zp, zs], axis=0)
            x = _matmul(z, bf(rwkv_wo[l]), name="rwkv_out_ln", extras=(x,), vecs=(g0, b0),
                        epilogue=_ln_epilogue(alpha), tm=256, tn=d)
        else:
            i = l - n_a
            q = _matmul(x, attn_wq, layer=i, name="attn_q", out_dtype=BF16,
                        epilogue=lambda acc: acc * (HEAD ** -0.5))
            table = attn_rel_bias[i]
            op = _attn_prompt(q, k_sh, v_sh, _prompt_bias(table, qb), n_seq=bp, seq_len=tp, qb=qb)
            bias_c, bias_n = _sample_bias(table, kb, ts)
            os_ = _attn_sample(q, k_sh, v_sh, cache_k.reshape(bs, kb, d), cache_v.reshape(bs, kb, d),
                               bias_c, bias_n, n_seq=bs, seq_len=ts, row0=mp)
            o = jnp.concatenate([op, os_], axis=0)
            x = _matmul(o, bf(attn_wo[i]), name="attn_out_ln", extras=(x,), vecs=(g0, b0),
                        epilogue=_ln_epilogue(alpha), tm=256, tn=d)
        j = l // 2
        if l % 2 == 0:
            x = _dense_ffn(x, ffn_w1[j], ffn_w3[j], ffn_w2[j], g1, b1, alpha=alpha, tm=tm_dense)
        else:
            x = _moe(x, moe_router[j], moe_w1, moe_w3, moe_w2, j, g1, b1, alpha=alpha, tm=tm_moe)
        if l == n_a - 1:
            k_sh = _matmul(x, attn_wk, name="attn_k")
            v_sh = _matmul(x, attn_wv, name="attn_v")

    heads = d // HEAD
    keep = min(PREV, tp)
    kp, vp = _to_heads((k_sh, v_sh), row0=tp - keep, seq_stride=tp, n_seq=bp, rows=keep)
    ks, vs = _to_heads((k_sh, v_sh), row0=mp, seq_stride=ms, n_seq=1, rows=ms)
    return (x[:mp].reshape(bp, tp, d), x[mp:].reshape(bs, ts, d),
            jnp.stack(p_wkv), jnp.stack(p_shift),
            kp.reshape(bp, keep, heads, HEAD), vp.reshape(bp, keep, heads, HEAD),
            jnp.stack(s_wkv), jnp.stack(s_shift),
            ks.reshape(bs, ts, heads, HEAD), vs.reshape(bs, ts, heads, HEAD))
```
